```python
import math
import jax, jax.numpy as jnp
from jax import lax
import numpy as np

D_MODEL = 1024
BATCH = 2
SEQ = 16384
DEPTH = 1
DEC_BATCH = 128
DEC_SEQ = 8
PAST_LEN = 8192
PAGE_SIZE = 128

ATTN_WIDTH = D_MODEL // 2
HEAD_DIM = 64
V_DIM = 2 * HEAD_DIM
N_HEADS = ATTN_WIDTH // V_DIM
QK_WIDTH = N_HEADS * 2 * HEAD_DIM
V_WIDTH = N_HEADS * V_DIM
CONV_DIM = D_MODEL - ATTN_WIDTH
CONV_W = 31
IN_COLS = 2 * QK_WIDTH + V_WIDTH + 2 * CONV_DIM
N_BUCKETS = 32
MAX_DISTANCE = 128
N_GROUPS = 8
EXPERTS_PER_GROUP = 8
N_EXPERTS = N_GROUPS * EXPERTS_PER_GROUP
TOP_K = 2
EXPERT_HIDDEN = 512
MOE_BLOCK = 128
Q_BLOCK = 128
EPS = 1e-6

kernel_name = "hybrid_diffattn_conformer_hmoe_step"


def rms_norm(x, w):
    x32 = x.astype(jnp.float32)
    y = x32 * lax.rsqrt(jnp.mean(x32 * x32, axis=-1, keepdims=True) + EPS)
    return y.astype(x.dtype) * w


def t5_bucket(dist):
    n = jnp.maximum(dist, 0)
    max_exact = N_BUCKETS // 2
    nf = jnp.maximum(n, 1).astype(jnp.float32)
    large = max_exact + (jnp.log(nf / max_exact) / math.log(MAX_DISTANCE / max_exact)
                         * (N_BUCKETS - max_exact)).astype(jnp.int32)
    large = jnp.minimum(large, N_BUCKETS - 1)
    return jnp.where(n < max_exact, n, large)


def diff_attend(q, k, v, q_pos, k_pos, lam, rel_bias):
    s = jnp.einsum('bqhmd,bkhmd->bhmqk', q, k).astype(jnp.float32) * (HEAD_DIM ** -0.5)
    dist = q_pos[:, None] - k_pos[None, :]
    bias = rel_bias[t5_bucket(dist)].astype(jnp.float32)
    s = s + jnp.transpose(bias, (2, 0, 1))[None, :, None]
    s = jnp.where(dist >= 0, s, jnp.finfo(jnp.float32).min)
    p = jax.nn.softmax(s, axis=-1)
    a = p[:, :, 0] - lam * p[:, :, 1]
    return jnp.einsum('bhqk,bkhv->bqhv', a.astype(v.dtype), v)


def project(x, norm1_w, w_in, q_norm_w, k_norm_w):
    b, s = x.shape[:2]
    z = rms_norm(x, norm1_w) @ w_in
    q, k, v, g = jnp.split(z, [QK_WIDTH, 2 * QK_WIDTH, 2 * QK_WIDTH + V_WIDTH], axis=-1)
    q = rms_norm(q.reshape(b, s, N_HEADS, 2, HEAD_DIM), q_norm_w)
    k = rms_norm(k.reshape(b, s, N_HEADS, 2, HEAD_DIM), k_norm_w)
    v = v.reshape(b, s, N_HEADS, V_DIM)
    u = g[..., :CONV_DIM] * jax.nn.sigmoid(g[..., CONV_DIM:])
    return q, k, v, u


def prompt_attention(q, k, v, lam, rel_bias):
    b, s = q.shape[:2]
    nqb = s // Q_BLOCK
    qb = q.reshape(b, nqb, Q_BLOCK, N_HEADS, 2, HEAD_DIM).swapaxes(0, 1)
    k_pos = jnp.arange(s, dtype=jnp.int32)

    def one(args):
        qblk, i = args
        q_pos = i * Q_BLOCK + jnp.arange(Q_BLOCK, dtype=jnp.int32)
        return diff_attend(qblk, k, v, q_pos, k_pos, lam, rel_bias)

    o = lax.map(one, (qb, jnp.arange(nqb, dtype=jnp.int32)))
    return o.swapaxes(0, 1).reshape(b, s, N_HEADS, V_DIM)


def sample_attention(q, k, v, cache_k, cache_v, layer, page_table, lam, rel_bias):
    ds = q.shape[1]
    past = page_table.shape[1] * PAGE_SIZE
    k_pos = jnp.arange(past + ds, dtype=jnp.int32)
    q_pos = past + jnp.arange(ds, dtype=jnp.int32)

    def one(args):
        qs, ks, vs, pages = args
        kp = cache_k[layer, pages].reshape(past, N_HEADS, 2, HEAD_DIM).astype(ks.dtype)
        vp = cache_v[layer, pages].reshape(past, N_HEADS, V_DIM).astype(vs.dtype)
        k_all = jnp.concatenate([kp, ks], axis=0)[None]
        v_all = jnp.concatenate([vp, vs], axis=0)[None]
        return diff_attend(qs[None], k_all, v_all, q_pos, k_pos, lam, rel_bias)[0]

    return lax.map(one, (q, k, v, page_table))


def conv_module(u, prev, conv_w, conv_b, ln_w, ln_b):
    up = jnp.concatenate([prev.astype(u.dtype), u], axis=1)
    c = lax.conv_general_dilated(up, conv_w[:, None, :].astype(u.dtype), window_strides=(1,),
                                 padding='VALID', dimension_numbers=('NWC', 'WIO', 'NWC'),
                                 feature_group_count=CONV_DIM) + conv_b
    c32 = c.astype(jnp.float32)
    mu = jnp.mean(c32, axis=-1, keepdims=True)
    var = jnp.mean(jnp.square(c32 - mu), axis=-1, keepdims=True)
    cn = ((c32 - mu) * lax.rsqrt(var + EPS)).astype(u.dtype) * ln_w + ln_b
    return jax.nn.silu(cn), up[:, -(CONV_W - 1):]


def hier_moe(h, w_group, b_group, w_expert, b_expert, w1, w3, w2):
    t, d = h.shape
    gp = jax.nn.softmax((h @ w_group + b_group).astype(jnp.float32), axis=-1)
    g_prob, g_idx = lax.top_k(gp, 1)
    el = (h @ w_expert + b_expert).astype(jnp.float32).reshape(t, N_GROUPS, EXPERTS_PER_GROUP)
    el = jnp.take_along_axis(el, g_idx[:, :, None], axis=1)[:, 0]
    e_val, e_idx = lax.top_k(el, TOP_K)
    e_prob = jax.nn.softmax(e_val, axis=-1)
    gates = (g_prob * e_prob).reshape(-1)
    expert_ids = (g_idx * EXPERTS_PER_GROUP + e_idx).reshape(-1).astype(jnp.int32)
    token_ids = jnp.repeat(jnp.arange(t, dtype=jnp.int32), TOP_K)
    n_assign = t * TOP_K
    order = jnp.argsort(expert_ids)
    e_sorted = expert_ids[order]
    counts = jnp.bincount(expert_ids, length=N_EXPERTS).astype(jnp.int32)
    padded = (counts + MOE_BLOCK - 1) // MOE_BLOCK * MOE_BLOCK
    pad_end = jnp.cumsum(padded)
    pad_start = pad_end - padded
    start = jnp.cumsum(counts) - counts
    dest = pad_start[e_sorted] + jnp.arange(n_assign, dtype=jnp.int32) - start[e_sorted]
    n_blocks = -(-n_assign // MOE_BLOCK) + N_EXPERTS
    n_slots = n_blocks * MOE_BLOCK
    slot_token = jnp.full((n_slots,), t, jnp.int32).at[dest].set(token_ids[order])
    slot_gate = jnp.zeros((n_slots,), jnp.float32).at[dest].set(gates[order])
    block_expert = jnp.minimum(
        jnp.searchsorted(pad_end, jnp.arange(n_blocks, dtype=jnp.int32) * MOE_BLOCK, side='right'),
        N_EXPERTS - 1)
    h_pad = jnp.concatenate([h, jnp.zeros((1, d), h.dtype)], axis=0)
    xb = h_pad[slot_token].reshape(n_blocks, MOE_BLOCK, d)

    def expert_block(args):
        xblk, e = args
        return (jax.nn.silu(xblk @ w1[e]) * (xblk @ w3[e])) @ w2[e]

    yb = lax.map(expert_block, (xb, block_expert))
    y = yb.reshape(n_slots, d) * slot_gate[:, None].astype(h.dtype)
    return jnp.zeros((t + 1, d), h.dtype).at[slot_token].add(y)[:t]


def merge_and_ffn(x, o_attn, o_conv, subln_w, lam_init, w_out, norm2_w,
                  w_group, b_group, w_expert, b_expert, w1, w3, w2):
    b, s = x.shape[:2]
    a = (rms_norm(o_attn, subln_w) * (1.0 - lam_init)).reshape(b, s, ATTN_WIDTH)
    x = x + jnp.concatenate([a, o_conv], axis=-1) @ w_out
    h = rms_norm(x, norm2_w).reshape(b * s, D_MODEL)
    return x + hier_moe(h, w_group, b_group, w_expert, b_expert, w1, w3, w2).reshape(b, s, D_MODEL)


def setup_inputs(seed: int = 0) -> dict:
    key = jax.random.key(seed)
    ks = jax.random.split(key, 32)
    f32 = jnp.float32
    n_pages = PAST_LEN // PAGE_SIZE
    n_used = DEC_BATCH * n_pages
    n_pool = n_used + n_used // 4

    def nrm(k, shape, scale=1.0):
        return jax.random.normal(k, shape, f32) * scale

    page_table = jax.random.permutation(ks[0], n_pool)[:n_used].reshape(DEC_BATCH, n_pages).astype(jnp.int32)
    return {
        "x_prompt": nrm(ks[1], (BATCH, SEQ, D_MODEL)),
        "x_sample": nrm(ks[2], (DEC_BATCH, DEC_SEQ, D_MODEL)),
        "cache_k": nrm(ks[3], (DEPTH, n_pool, PAGE_SIZE, N_HEADS, 2 * HEAD_DIM)),
        "cache_v": nrm(ks[4], (DEPTH, n_pool, PAGE_SIZE, N_HEADS, V_DIM)),
        "state_conv": nrm(ks[5], (DEPTH, DEC_BATCH, CONV_W - 1, CONV_DIM), 0.5),
        "page_table": page_table,
        "rel_bias": nrm(ks[6], (N_BUCKETS, N_HEADS), 0.5),
        "norm1_w": 1.0 + nrm(ks[7], (DEPTH, D_MODEL), 0.1),
        "w_in": nrm(ks[8], (DEPTH, D_MODEL, IN_COLS), D_MODEL ** -0.5),
        "q_norm_w": 1.0 + nrm(ks[9], (DEPTH, HEAD_DIM), 0.1),
        "k_norm_w": 1.0 + nrm(ks[10], (DEPTH, HEAD_DIM), 0.1),
        "lambda_q1": nrm(ks[11], (DEPTH, HEAD_DIM), 0.1),
        "lambda_k1": nrm(ks[12], (DEPTH, HEAD_DIM), 0.1),
        "lambda_q2": nrm(ks[13], (DEPTH, HEAD_DIM), 0.1),
        "lambda_k2": nrm(ks[14], (DEPTH, HEAD_DIM), 0.1),
        "subln_w": 1.0 + nrm(ks[15], (DEPTH, V_DIM), 0.1),
        "conv_w": nrm(ks[16], (DEPTH, CONV_W, CONV_DIM), CONV_W ** -0.5),
        "conv_b": nrm(ks[17], (DEPTH, CONV_DIM), 0.02),
        "conv_ln_w": 1.0 + nrm(ks[18], (DEPTH, CONV_DIM), 0.1),
        "conv_ln_b": nrm(ks[19], (DEPTH, CONV_DIM), 0.02),
        "w_out": nrm(ks[20], (DEPTH, D_MODEL, D_MODEL), D_MODEL ** -0.5),
        "norm2_w": 1.0 + nrm(ks[21], (DEPTH, D_MODEL), 0.1),
        "w_group": nrm(ks[22], (DEPTH, D_MODEL, N_GROUPS), D_MODEL ** -0.5),
        "b_group": nrm(ks[23], (DEPTH, N_GROUPS), 0.01),
        "w_expert": nrm(ks[24], (DEPTH, D_MODEL, N_EXPERTS), D_MODEL ** -0.5),
        "b_expert": nrm(ks[25], (DEPTH, N_EXPERTS), 0.01),
        "w1": nrm(ks[26], (DEPTH, N_EXPERTS, D_MODEL, EXPERT_HIDDEN), D_MODEL ** -0.5),
        "w3": nrm(ks[27], (DEPTH, N_EXPERTS, D_MODEL, EXPERT_HIDDEN), D_MODEL ** -0.5),
        "w2": nrm(ks[28], (DEPTH, N_EXPERTS, EXPERT_HIDDEN, D_MODEL), EXPERT_HIDDEN ** -0.5),
    }


def reference(x_prompt, x_sample, cache_k, cache_v, state_conv, page_table, rel_bias,
              norm1_w, w_in, q_norm_w, k_norm_w, lambda_q1, lambda_k1, lambda_q2, lambda_k2,
              subln_w, conv_w, conv_b, conv_ln_w, conv_ln_b, w_out, norm2_w,
              w_group, b_group, w_expert, b_expert, w1, w3, w2):
    xp, xs = x_prompt, x_sample
    bp = xp.shape[0]
    kp_l, vp_l, cp_l, ks_l, vs_l, cs_l = [], [], [], [], [], []
    for l in range(DEPTH):
        lam_init = 0.8 - 0.6 * math.exp(-0.3 * l)
        lam = (jnp.exp(jnp.sum(lambda_q1[l].astype(jnp.float32) * lambda_k1[l].astype(jnp.float32)))
               - jnp.exp(jnp.sum(lambda_q2[l].astype(jnp.float32) * lambda_k2[l].astype(jnp.float32)))
               + lam_init)
        ffn_args = (subln_w[l], lam_init, w_out[l], norm2_w[l], w_group[l], b_group[l],
                    w_expert[l], b_expert[l], w1[l], w3[l], w2[l])
        q, k, v, u = project(xp, norm1_w[l], w_in[l], q_norm_w[l], k_norm_w[l])
        o_attn = prompt_attention(q, k, v, lam, rel_bias)
        o_conv, conv_new = conv_module(u, jnp.zeros((bp, CONV_W - 1, CONV_DIM), u.dtype),
                                       conv_w[l], conv_b[l], conv_ln_w[l], conv_ln_b[l])
        kp_l.append(k.reshape(k.shape[0], k.shape[1], N_HEADS, 2 * HEAD_DIM))
        vp_l.append(v)
        cp_l.append(conv_new)
        xp = merge_and_ffn(xp, o_attn, o_conv, *ffn_args)
        q, k, v, u = project(xs, norm1_w[l], w_in[l], q_norm_w[l], k_norm_w[l])
        o_attn = sample_attention(q, k, v, cache_k, cache_v, l, page_table, lam, rel_bias)
        o_conv, conv_new = conv_module(u, state_conv[l], conv_w[l], conv_b[l], conv_ln_w[l], conv_ln_b[l])
        ks_l.append(k.reshape(k.shape[0], k.shape[1], N_HEADS, 2 * HEAD_DIM))
        vs_l.append(v)
        cs_l.append(conv_new)
        xs = merge_and_ffn(xs, o_attn, o_conv, *ffn_args)
    k_prompt = jnp.stack(kp_l)
    v_prompt = jnp.stack(vp_l)
    conv_prompt = jnp.stack(cp_l)
    k_sample = jnp.stack(ks_l)
    v_sample = jnp.stack(vs_l)
    conv_sample = jnp.stack(cs_l)
    return (xp, xs, k_prompt, v_prompt, conv_prompt, k_sample, v_sample, conv_sample)
```

```python
import functools
import math

import numpy as np
import jax
import jax.numpy as jnp
from jax import lax
from jax.experimental import pallas as pl
from jax.experimental.pallas import tpu as pltpu

F32 = jnp.float32
BF16 = jnp.bfloat16

N_HEADS = 4
HEAD_DIM = 64
V_DIM = 2 * HEAD_DIM
ATTN_WIDTH = N_HEADS * V_DIM
CONV_DIM = 512
CONV_W = 31
N_BUCKETS = 32
MAX_DISTANCE = 128
N_GROUPS = 8
EXPERTS_PER_GROUP = 8
N_EXPERTS = N_GROUPS * EXPERTS_PER_GROUP
PAGE_SIZE = 128
EPS = 1e-6
MASKED = -1e30

VMEM_LIMIT = 56 * 1024 * 1024


def _cparams(sem):
    return pltpu.CompilerParams(dimension_semantics=sem, vmem_limit_bytes=VMEM_LIMIT)


PROJ_TM = 512
KV_TILE = 256


def _proj_kernel(x_ref, n1_ref, win_ref, qwc_ref, kw_ref, g_ref,
                 qt_ref, kb_ref, k32_ref, vt_ref, v32_ref, u_ref):
    tm = x_ref.shape[0]
    x = x_ref[...]
    ms = jnp.mean(x * x, axis=-1, keepdims=True)
    xn = (x * lax.rsqrt(ms + EPS)) * n1_ref[...]
    z = jnp.dot(xn.astype(BF16), win_ref[...], preferred_element_type=F32)
    w = ATTN_WIDTH
    zq, zk, zv = z[:, :w], z[:, w:2 * w], z[:, 2 * w:3 * w]
    g1, g2 = z[:, 3 * w:3 * w + CONV_DIM], z[:, 3 * w + CONV_DIM:]

    kk = zk * zk
    hi = kk.astype(BF16)
    lo = (kk - hi.astype(F32)).astype(BF16)
    kms = (jnp.dot(hi, g_ref[...], preferred_element_type=F32)
           + jnp.dot(lo, g_ref[...], preferred_element_type=F32))
    kn = (zk * lax.rsqrt(kms + EPS)) * kw_ref[...]
    k32_ref[...] = kn
    kb_ref[...] = kn.astype(BF16)

    v32_ref[...] = zv
    zvt = zv.T
    for t in range(tm // KV_TILE):
        vt_ref[t] = zvt[:, t * KV_TILE:(t + 1) * KV_TILE].astype(BF16)

    q3 = zq.T.reshape(2 * N_HEADS, HEAD_DIM, tm)
    qms = jnp.mean(q3 * q3, axis=1, keepdims=True)
    qn = (q3 * lax.rsqrt(qms + EPS)).reshape(w, tm) * qwc_ref[...]
    qt_ref[...] = qn.astype(BF16)

    u_ref[...] = g1 * (1.0 / (1.0 + jnp.exp(-g2)))


def _proj(x2d, n1, win_b, qwc, kw, gmat):
    t = x2d.shape[0]
    tm = PROJ_TM
    assert t % tm == 0
    w = ATTN_WIDTH
    const = lambda shape: pl.BlockSpec(shape, lambda i: (0,) * len(shape))
    return pl.pallas_call(
        _proj_kernel,
        grid=(t // tm,),
        in_specs=[
            pl.BlockSpec((tm, x2d.shape[1]), lambda i: (i, 0)),
            const(n1.shape), const(win_b.shape), const(qwc.shape), const(kw.shape), const(gmat.shape),
        ],
        out_specs=[
            pl.BlockSpec((w, tm), lambda i: (0, i)),
            pl.BlockSpec((tm, w), lambda i: (i, 0)),
            pl.BlockSpec((tm, w), lambda i: (i, 0)),
            pl.BlockSpec((tm // KV_TILE, w, KV_TILE), lambda i: (i, 0, 0)),
            pl.BlockSpec((tm, w), lambda i: (i, 0)),
            pl.BlockSpec((tm, CONV_DIM), lambda i: (i, 0)),
        ],
        out_shape=[
            jax.ShapeDtypeStruct((w, t), BF16),
            jax.ShapeDtypeStruct((t, w), BF16),
            jax.ShapeDtypeStruct((t, w), F32),
            jax.ShapeDtypeStruct((t // KV_TILE, w, KV_TILE), BF16),
            jax.ShapeDtypeStruct((t, w), F32),
            jax.ShapeDtypeStruct((t, CONV_DIM), F32),
        ],
        compiler_params=_cparams(("arbitrary",)),
        name="proj",
    )(x2d, n1, win_b, qwc, kw, gmat)


CONV_TC = 512
CONV_RC = 64
CONV_HALO = 32


def _ln_swish(c, lnw, lnb):
    mu = jnp.mean(c, axis=-1, keepdims=True)
    d = c - mu
    var = jnp.mean(d * d, axis=-1, keepdims=True)
    cn = (d * lax.rsqrt(var + EPS)) * lnw + lnb
    return cn * (1.0 / (1.0 + jnp.exp(-cn)))


def _pconv_kernel(prev_ref, cur_ref, w_ref, b_ref, lnw_ref, lnb_ref, o_ref, win_ref):
    i = pl.program_id(1)
    tc = cur_ref.shape[0]
    n = CONV_HALO + tc
    win_ref[0, 0:CONV_HALO, :] = jnp.where(i > 0, prev_ref[...], 0.0)
    win_ref[0, CONV_HALO:n, :] = cur_ref[...]
    win_ref[0, n:n + 8, :] = jnp.zeros((8, CONV_DIM), F32)
    step = n // 4
    for r in range(1, 8):
        for c0 in range(0, n, step):
            win_ref[r, c0:c0 + step, :] = win_ref[0, c0 + r:c0 + r + step, :]
    off = CONV_HALO - (CONV_W - 1)

    def chunk(c, carry):
        base = pl.multiple_of(c * CONV_RC, CONV_RC)
        acc = jnp.zeros((CONV_RC, CONV_DIM), F32) + b_ref[...]
        for k in range(CONV_W):
            shift = k + off
            rows = pl.ds(pl.multiple_of(base + (shift // 8) * 8, 8), CONV_RC)
            acc = acc + w_ref[k:k + 1, :] * win_ref[shift % 8, rows, :]
        o_ref[pl.ds(base, CONV_RC), :] = _ln_swish(acc, lnw_ref[...], lnb_ref[...]).astype(o_ref.dtype)
        return carry

    lax.fori_loop(0, tc // CONV_RC, chunk, 0)


def _pconv(u2d, batch, seq, cw, cb, lnw, lnb):
    tc = CONV_TC
    assert seq % tc == 0
    nt = seq // tc
    const = lambda shape: pl.BlockSpec(shape, lambda b, i: (0,) * len(shape))
    return pl.pallas_call(
        _pconv_kernel,
        grid=(batch, nt),
        in_specs=[
            pl.BlockSpec((CONV_HALO, CONV_DIM),
                         lambda b, i: (jnp.maximum((b * seq + i * tc) // CONV_HALO - 1, 0), 0)),
            pl.BlockSpec((tc, CONV_DIM), lambda b, i: (b * nt + i, 0)),
            const(cw.shape), const(cb.shape), const(lnw.shape), const(lnb.shape),
        ],
        out_specs=pl.BlockSpec((tc, CONV_DIM), lambda b, i: (b * nt + i, 0)),
        out_shape=jax.ShapeDtypeStruct((batch * seq, CONV_DIM), BF16),
        scratch_shapes=[pltpu.VMEM((8, CONV_HALO + tc + 8, CONV_DIM), F32)],
        compiler_params=_cparams(("arbitrary", "arbitrary")),
        name="pconv",
    )(u2d, u2d, cw, cb, lnw, lnb)


SCONV_R = 8


def _sconv_kernel(st_ref, u_ref, w_ref, b_ref, lnw_ref, lnb_ref, o_ref, ns_ref, win_ref):
    nprev = st_ref.shape[1]
    ds = u_ref.shape[1]
    for r in range(st_ref.shape[0]):
        win_ref[0:nprev, :] = st_ref[r]
        win_ref[nprev:nprev + ds, :] = u_ref[r]
        acc = jnp.zeros((ds, CONV_DIM), F32) + b_ref[...]
        for k in range(CONV_W):
            acc = acc + w_ref[k:k + 1, :] * win_ref[k:k + ds, :]
        o_ref[r] = _ln_swish(acc, lnw_ref[...], lnb_ref[...])
        ns_ref[r] = win_ref[ds:ds + nprev, :]


def _sconv(state, u3, cw, cb, lnw, lnb):
    db, nprev, _ = state.shape
    ds = u3.shape[1]
    r = SCONV_R
    assert db % r == 0 and nprev == CONV_W - 1
    const = lambda shape: pl.BlockSpec(shape, lambda i: (0,) * len(shape))
    return pl.pallas_call(
        _sconv_kernel,
        grid=(db // r,),
        in_specs=[
            pl.BlockSpec((r, nprev, CONV_DIM), lambda i: (i, 0, 0)),
            pl.BlockSpec((r, ds, CONV_DIM), lambda i: (i, 0, 0)),
            const(cw.shape), const(cb.shape), const(lnw.shape), const(lnb.shape),
        ],
        out_specs=[
            pl.BlockSpec((r, ds, CONV_DIM), lambda i: (i, 0, 0)),
            pl.BlockSpec((r, nprev, CONV_DIM), lambda i: (i, 0, 0)),
        ],
        out_shape=[
            jax.ShapeDtypeStruct((db, ds, CONV_DIM), F32),
            jax.ShapeDtypeStruct((db, nprev, CONV_DIM), F32),
        ],
        scratch_shapes=[pltpu.VMEM((nprev + ds + 2, CONV_DIM), F32)],
        compiler_params=_cparams(("arbitrary",)),
        name="sconv",
    )(state, u3, cw, cb, lnw, lnb)


PATTN_TQ = 2 * KV_TILE


def _pattn_kernel(lam_ref, qt_ref, kb_ref, vt_ref, bias_ref, sw_ref, o_ref, m_ref, l_ref, acc_ref,
                  *, lam_init):
    i = pl.program_id(2)
    tq = qt_ref.shape[1]
    tk = KV_TILE
    r = tq // tk
    m_ref[...] = jnp.full(m_ref.shape, -jnp.inf, F32)
    l_ref[...] = jnp.zeros(l_ref.shape, F32)
    acc_ref[...] = jnp.zeros(acc_ref.shape, F32)

    def tile(j, bias):
        k0 = pl.multiple_of(j * tk, tk)
        for mm in range(2):
            kt = kb_ref[pl.ds(k0, tk), mm * HEAD_DIM:(mm + 1) * HEAD_DIM]
            s = jnp.dot(kt, qt_ref[mm * HEAD_DIM:(mm + 1) * HEAD_DIM, :], preferred_element_type=F32)
            if bias is not None:
                s = s + bias
            m_prev = m_ref[mm]
            m_new = jnp.maximum(m_prev, jnp.max(s, axis=0, keepdims=True))
            alpha = jnp.exp(m_prev - m_new)
            p = jnp.exp(s - m_new)
            l_ref[mm] = alpha * l_ref[mm] + jnp.sum(p, axis=0, keepdims=True)
            pv = jnp.dot(vt_ref[j], p.astype(BF16), preferred_element_type=F32)
            acc_ref[mm] = alpha * acc_ref[mm] + pv
            m_ref[mm] = m_new

    def far(j, carry):
        tile(j, None)
        return carry

    lax.fori_loop(0, jnp.maximum(r * i - 1, 0), far, 0)

    @pl.when(i > 0)
    def _():
        tile(r * i - 1, bias_ref[0])

    for c in range(r):
        tile(r * i + c, bias_ref[1 + c])

    lam = lam_ref[0]
    o = acc_ref[0] / l_ref[0] - lam * (acc_ref[1] / l_ref[1])
    ms = jnp.mean(o * o, axis=0, keepdims=True)
    on = ((o * lax.rsqrt(ms + EPS)) * sw_ref[...]) * (1.0 - lam_init)
    o_ref[...] = on.T.astype(o_ref.dtype)


def _pattn(lam, qt, kb, vt, bias_tiles, swc, batch, seq, lam_init):
    tq = PATTN_TQ
    tk = KV_TILE
    assert seq % tq == 0
    nq = seq // tq
    nkv = seq // tk
    return pl.pallas_call(
        functools.partial(_pattn_kernel, lam_init=lam_init),
        grid=(batch, N_HEADS, nq),
        in_specs=[
            pl.BlockSpec(memory_space=pltpu.SMEM),
            pl.BlockSpec((V_DIM, tq), lambda b, h, i: (h, b * nq + i)),
            pl.BlockSpec((seq, V_DIM), lambda b, h, i: (b, h)),
            pl.BlockSpec((nkv, V_DIM, tk), lambda b, h, i: (b, h, 0)),
            pl.BlockSpec((None, 1 + tq // tk, tk, tq), lambda b, h, i: (h, 0, 0, 0)),
            pl.BlockSpec(swc.shape, lambda b, h, i: (0, 0)),
        ],
        out_specs=pl.BlockSpec((tq, V_DIM), lambda b, h, i: (b * nq + i, h)),
        out_shape=jax.ShapeDtypeStruct((batch * seq, ATTN_WIDTH), BF16),
        scratch_shapes=[
            pltpu.VMEM((2, 1, tq), F32),
            pltpu.VMEM((2, 1, tq), F32),
            pltpu.VMEM((2, V_DIM, tq), F32),
        ],
        compiler_params=_cparams(("arbitrary", "arbitrary", "arbitrary")),
        name="pattn",
    )(lam, qt, kb, vt, bias_tiles, swc)


SATTN_CP = 8
SATTN_NBUF = 4
ROWS_PER_PAGE = PAGE_SIZE * N_HEADS


def _sattn_kernel(pt_ref, lam_ref, q_ref, kn_ref, vn_ref, hm_ref, blast_ref, bnew_ref, sw_ref,
                  ck_ref, cv_ref, o_ref, kbuf, vbuf, ksem, vsem, *, n_req, n_chunks, lam_init):
    r = pl.program_id(0)
    cp = SATTN_CP
    nbuf = SATTN_NBUF
    rows = cp * ROWS_PER_PAGE

    def page_copies(req, chunk, slot):
        out = []
        for p in range(cp):
            pg = pt_ref[req, chunk * cp + p]
            out.append(pltpu.make_async_copy(ck_ref.at[0, pg], kbuf.at[slot, p], ksem.at[slot]))
            out.append(pltpu.make_async_copy(cv_ref.at[0, pg], vbuf.at[slot, p], vsem.at[slot]))
        return out

    def start_chunk(g):
        req = g // n_chunks
        chunk = g % n_chunks
        for c in page_copies(req, chunk, g % nbuf):
            c.start()

    @pl.when(r == 0)
    def _():
        for g in range(nbuf - 1):
            start_chunk(g)

    qs = q_ref[0]
    ds = qs.shape[0]
    lane = lax.broadcasted_iota(jnp.int32, (ds, V_DIM), 1)
    pieces = []
    for h in range(N_HEADS):
        qh = qs[:, h * V_DIM:(h + 1) * V_DIM]
        for mm in range(2):
            keep = (lane < HEAD_DIM) if mm == 0 else (lane >= HEAD_DIM)
            pieces.append(jnp.where(keep, qh, 0.0))
    q2 = jnp.concatenate(pieces, axis=0).astype(BF16)
    nrow = q2.shape[0]

    def step(s, state, v2):
        m_prev, l_prev, acc = state
        m_new = jnp.maximum(m_prev, jnp.max(s, axis=1, keepdims=True))
        alpha = jnp.exp(m_prev - m_new)
        p = jnp.exp(s - m_new)
        l_new = alpha * l_prev + jnp.sum(p, axis=1, keepdims=True)
        acc_new = alpha * acc + jnp.dot(p.astype(BF16), v2, preferred_element_type=F32)
        return m_new, l_new, acc_new

    nt = (((1,), (1,)), ((), ()))
    hm = hm_ref[...]
    hm_chunk = jnp.concatenate([hm] * cp, axis=1)
    last_mask = jnp.concatenate([hm] * (cp - 1) + [blast_ref[...]], axis=1)

    def chunk_body(c, state):
        g = r * n_chunks + c
        nxt = g + (nbuf - 1)

        @pl.when(nxt < n_req * n_chunks)
        def _():
            start_chunk(nxt)

        slot = g % nbuf
        for cpy in page_copies(r, c, slot):
            cpy.wait()
        k2 = kbuf[slot].reshape(rows, V_DIM).astype(BF16)
        v2 = vbuf[slot].reshape(rows, V_DIM).astype(BF16)
        s = lax.dot_general(q2, k2, nt, preferred_element_type=F32)
        s = s + jnp.where(c == n_chunks - 1, last_mask, hm_chunk)
        return step(s, state, v2)

    state = (jnp.full((nrow, 1), -jnp.inf, F32), jnp.zeros((nrow, 1), F32), jnp.zeros((nrow, V_DIM), F32))
    state = lax.fori_loop(0, n_chunks, chunk_body, state)

    kn = kn_ref[0].astype(BF16)
    vn = vn_ref[0].astype(BF16)
    s = lax.dot_general(q2, kn, nt, preferred_element_type=F32) + bnew_ref[...]
    _, l_fin, acc = step(s, state, vn)

    lam = lam_ref[0]
    o = acc / l_fin
    outs = []
    for h in range(N_HEADS):
        o1 = o[(2 * h) * ds:(2 * h + 1) * ds]
        o2 = o[(2 * h + 1) * ds:(2 * h + 2) * ds]
        oh = o1 - lam * o2
        ms = jnp.mean(oh * oh, axis=-1, keepdims=True)
        outs.append(((oh * lax.rsqrt(ms + EPS)) * sw_ref[...]) * (1.0 - lam_init))
    o_ref[0] = jnp.concatenate(outs, axis=1)


def _sattn(page_table, lam, q3, kn3, vn3, hm, blast, bnew, swr, cache_k, cache_v, lam_init):
    n_req, ds, _ = q3.shape
    n_pages = page_table.shape[1]
    assert n_pages % SATTN_CP == 0
    n_chunks = n_pages // SATTN_CP
    assert n_chunks % SATTN_NBUF == 0 or SATTN_NBUF % n_chunks == 0 or True
    nrow = 2 * N_HEADS * ds
    rows = SATTN_CP * ROWS_PER_PAGE
    kernel = functools.partial(_sattn_kernel, n_req=n_req, n_chunks=n_chunks, lam_init=lam_init)
    const = lambda shape: pl.BlockSpec(shape, lambda i, pt: (0,) * len(shape))
    return pl.pallas_call(
        kernel,
        grid_spec=pltpu.PrefetchScalarGridSpec(
            num_scalar_prefetch=1,
            grid=(n_req,),
            in_specs=[
                pl.BlockSpec(memory_space=pltpu.SMEM),
                pl.BlockSpec((1, ds, ATTN_WIDTH), lambda i, pt: (i, 0, 0)),
                pl.BlockSpec((1, ds * N_HEADS, V_DIM), lambda i, pt: (i, 0, 0)),
                pl.BlockSpec((1, ds * N_HEADS, V_DIM), lambda i, pt: (i, 0, 0)),
                const(hm.shape), const(blast.shape), const(bnew.shape), const(swr.shape),
                pl.BlockSpec(memory_space=pl.ANY),
                pl.BlockSpec(memory_space=pl.ANY),
            ],
            out_specs=pl.BlockSpec((1, ds, ATTN_WIDTH), lambda i, pt: (i, 0, 0)),
            scratch_shapes=[
                pltpu.VMEM((SATTN_NBUF, SATTN_CP, PAGE_SIZE, N_HEADS, V_DIM), F32),
                pltpu.VMEM((SATTN_NBUF, SATTN_CP, PAGE_SIZE, N_HEADS, V_DIM), F32),
                pltpu.SemaphoreType.DMA((SATTN_NBUF,)),
                pltpu.SemaphoreType.DMA((SATTN_NBUF,)),
            ],
        ),
        out_shape=jax.ShapeDtypeStruct((n_req, ds, ATTN_WIDTH), F32),
        compiler_params=_cparams(("arbitrary",)),
        name="sattn",
    )(page_table, lam, q3, kn3, vn3, hm, blast, bnew, swr, cache_k, cache_v)


MIX_TM = 512
ROUTE_ROWS = 8


def _mix_kernel(a_ref, c_ref, x_ref, wo_ref, n2_ref, wr_ref, br_ref, tri_ref, cin_ref,
                x2_ref, h_ref, route_ref, cnt_ref):
    i = pl.program_id(0)
    tm = x_ref.shape[0]
    half = a_ref.shape[1]

    @pl.when(i == 0)
    def _():
        cnt_ref[...] = cin_ref[...]

    x2 = (x_ref[...]
          + jnp.dot(a_ref[...].astype(BF16), wo_ref[0:half, :], preferred_element_type=F32)
          + jnp.dot(c_ref[...].astype(BF16), wo_ref[half:, :], preferred_element_type=F32))
    x2_ref[...] = x2
    ms = jnp.mean(x2 * x2, axis=-1, keepdims=True)
    h = (x2 * lax.rsqrt(ms + EPS)) * n2_ref[...]
    h_ref[...] = h

    h_hi = h.astype(BF16)
    h_lo = (h - h_hi.astype(F32)).astype(BF16)
    wr = wr_ref[...]
    w_hi = wr.astype(BF16)
    w_lo = (wr - w_hi.astype(F32)).astype(BF16)
    nt = (((1,), (1,)), ((), ()))
    logits = (lax.dot_general(w_hi, h_hi, nt, preferred_element_type=F32)
              + lax.dot_general(w_hi, h_lo, nt, preferred_element_type=F32)
              + lax.dot_general(w_lo, h_hi, nt, preferred_element_type=F32)) + br_ref[...]
    gl = logits[0:N_GROUPS]
    el = logits[N_GROUPS:N_GROUPS + N_EXPERTS]

    sub8 = lax.broadcasted_iota(jnp.int32, (N_GROUPS, tm), 0)

    def first_argmax(vals):
        top = jnp.max(vals, axis=0, keepdims=True)
        idx = jnp.min(jnp.where(vals == top, sub8, N_GROUPS), axis=0, keepdims=True)
        return top, idx

    gexp = jnp.exp(gl - jnp.max(gl, axis=0, keepdims=True))
    gp = gexp / jnp.sum(gexp, axis=0, keepdims=True)
    g_prob, g_idx = first_argmax(gp)

    el3 = el.reshape(N_GROUPS, EXPERTS_PER_GROUP, tm)
    sel = jnp.zeros((EXPERTS_PER_GROUP, tm), F32)
    for g in range(N_GROUPS):
        sel = sel + jnp.where(g_idx == g, el3[g], 0.0)
    v1, i1 = first_argmax(sel)
    v2, i2 = first_argmax(jnp.where(sub8 == i1, -jnp.inf, sel))
    e2 = jnp.exp(v2 - v1)
    p1 = 1.0 / (1.0 + e2)
    p2 = e2 / (1.0 + e2)
    ea = g_idx * EXPERTS_PER_GROUP + i1
    eb = g_idx * EXPERTS_PER_GROUP + i2

    sub64 = lax.broadcasted_iota(jnp.int32, (N_EXPERTS, tm), 0)
    oh_a = sub64 == ea
    oh_b = sub64 == eb
    oh = jnp.where(oh_a | oh_b, 1.0, 0.0)
    before = jnp.dot(oh.astype(BF16), tri_ref[...], preferred_element_type=F32) + cnt_ref[...]
    rank_a = jnp.sum(jnp.where(oh_a, before, 0.0), axis=0, keepdims=True)
    rank_b = jnp.sum(jnp.where(oh_b, before, 0.0), axis=0, keepdims=True)
    cnt_ref[...] = cnt_ref[...] + jnp.sum(oh, axis=1, keepdims=True)

    route_ref[...] = jnp.concatenate(
        [ea.astype(F32), eb.astype(F32), rank_a, rank_b, g_prob * p1, g_prob * p2,
         jnp.zeros((ROUTE_ROWS - 6, tm), F32)], axis=0)


def _mix(a, c, x2d, wo_b, n2, wr, br, tri, cnt_in):
    t, d = x2d.shape
    tm = MIX_TM
    assert t % tm == 0
    const = lambda shape: pl.BlockSpec(shape, lambda i: (0,) * len(shape))
    return pl.pallas_call(
        _mix_kernel,
        grid=(t // tm,),
        in_specs=[
            pl.BlockSpec((tm, a.shape[1]), lambda i: (i, 0)),
            pl.BlockSpec((tm, c.shape[1]), lambda i: (i, 0)),
            pl.BlockSpec((tm, d), lambda i: (i, 0)),
            const(wo_b.shape), const(n2.shape), const(wr.shape), const(br.shape), const(tri.shape),
            const(cnt_in.shape),
        ],
        out_specs=[
            pl.BlockSpec((tm, d), lambda i: (i, 0)),
            pl.BlockSpec((tm, d), lambda i: (i, 0)),
            pl.BlockSpec((ROUTE_ROWS, tm), lambda i: (0, i)),
            const(cnt_in.shape),
        ],
        out_shape=[
            jax.ShapeDtypeStruct((t, d), F32),
            jax.ShapeDtypeStruct((t, d), F32),
            jax.ShapeDtypeStruct((ROUTE_ROWS, t), F32),
            jax.ShapeDtypeStruct(cnt_in.shape, F32),
        ],
        compiler_params=_cparams(("arbitrary",)),
        name="mix",
    )(a, c, x2d, wo_b, n2, wr, br, tri, cnt_in)


ROW_TM = 256
EXPERT_BM = 256


def _dispatch_kernel(slots_ref, h_ref, xin_ref, xb_ref, sem):
    del xin_ref
    tm = h_ref.shape[0]

    def issue(t, carry):
        for j in range(2):
            s = slots_ref[0, j, t]
            pltpu.make_async_copy(h_ref.at[pl.ds(t, 1)], xb_ref.at[pl.ds(s, 1)], sem).start()
        return carry

    lax.fori_loop(0, tm, issue, 0)

    def drain(t, carry):
        for j in range(2):
            pltpu.make_async_copy(h_ref.at[pl.ds(0, 1)], xb_ref.at[pl.ds(0, 1)], sem).wait()
        return carry

    lax.fori_loop(0, tm, drain, 0)


def _dispatch(slots3, h, xb):
    t, d = h.shape
    tm = ROW_TM
    assert t % tm == 0
    return pl.pallas_call(
        _dispatch_kernel,
        grid=(t // tm,),
        in_specs=[
            pl.BlockSpec((1, 2, tm), lambda i: (i, 0, 0), memory_space=pltpu.SMEM),
            pl.BlockSpec((tm, d), lambda i: (i, 0)),
            pl.BlockSpec(memory_space=pl.ANY),
        ],
        out_specs=pl.BlockSpec(memory_space=pl.ANY),
        out_shape=jax.ShapeDtypeStruct(xb.shape, xb.dtype),
        scratch_shapes=[pltpu.SemaphoreType.DMA],
        input_output_aliases={2: 0},
        compiler_params=_cparams(("arbitrary",)),
        name="dispatch",
    )(slots3, h, xb)


def _experts_kernel(be_ref, nb_ref, xb_ref, w1_ref, w3_ref, w2_ref, yb_ref):
    i = pl.program_id(0)

    @pl.when(i < nb_ref[0])
    def _():
        x = xb_ref[...].astype(BF16)
        a = jnp.dot(x, w1_ref[0], preferred_element_type=F32)
        b = jnp.dot(x, w3_ref[0], preferred_element_type=F32)
        mid = (a * (1.0 / (1.0 + jnp.exp(-a)))) * b
        yb_ref[...] = jnp.dot(mid.astype(BF16), w2_ref[0], preferred_element_type=F32)

    @pl.when(i >= nb_ref[0])
    def _():
        yb_ref[...] = jnp.zeros(yb_ref.shape, F32)


def _experts(block_expert, n_used, xb, w1b, w3b, w2b):
    nslot, d = xb.shape
    bm = EXPERT_BM
    hid = w1b.shape[2]
    return pl.pallas_call(
        _experts_kernel,
        grid_spec=pltpu.PrefetchScalarGridSpec(
            num_scalar_prefetch=2,
            grid=(nslot // bm,),
            in_specs=[
                pl.BlockSpec((bm, d), lambda i, be, nb: (i, 0)),
                pl.BlockSpec((1, d, hid), lambda i, be, nb: (be[i], 0, 0)),
                pl.BlockSpec((1, d, hid), lambda i, be, nb: (be[i], 0, 0)),
                pl.BlockSpec((1, hid, d), lambda i, be, nb: (be[i], 0, 0)),
            ],
            out_specs=pl.BlockSpec((bm, d), lambda i, be, nb: (i, 0)),
        ),
        out_shape=jax.ShapeDtypeStruct((nslot, d), F32),
        compiler_params=_cparams(("arbitrary",)),
        name="experts",
    )(block_expert, n_used, xb, w1b, w3b, w2b)


def _combine_kernel(slots_ref, x2_ref, g_ref, yb_ref, o_ref, ybuf, sem):
    tm = x2_ref.shape[0]

    def issue(t, carry):
        for j in range(2):
            s = slots_ref[0, j, t]
            pltpu.make_async_copy(yb_ref.at[pl.ds(s, 1)], ybuf.at[j, pl.ds(t, 1)], sem).start()
        return carry

    lax.fori_loop(0, tm, issue, 0)

    def drain(t, carry):
        for j in range(2):
            pltpu.make_async_copy(yb_ref.at[pl.ds(0, 1)], ybuf.at[j, pl.ds(0, 1)], sem).wait()
        return carry

    lax.fori_loop(0, tm, drain, 0)
    g = g_ref[...]
    o_ref[...] = x2_ref[...] + (g[:, 0:1] * ybuf[0] + g[:, 1:2] * ybuf[1])


def _combine(slots3, x2, gates, yb):
    t, d = x2.shape
    tm = ROW_TM
    assert t % tm == 0
    return pl.pallas_call(
        _combine_kernel,
        grid=(t // tm,),
        in_specs=[
            pl.BlockSpec((1, 2, tm), lambda i: (i, 0, 0), memory_space=pltpu.SMEM),
            pl.BlockSpec((tm, d), lambda i: (i, 0)),
            pl.BlockSpec((tm, 2), lambda i: (i, 0)),
            pl.BlockSpec(memory_space=pl.ANY),
        ],
        out_specs=pl.BlockSpec((tm, d), lambda i: (i, 0)),
        out_shape=jax.ShapeDtypeStruct((t, d), F32),
        scratch_shapes=[pltpu.VMEM((2, tm, d), F32), pltpu.SemaphoreType.DMA],
        compiler_params=_cparams(("arbitrary",)),
        name="combine",
    )(slots3, x2, gates, yb)


def _t5_bucket(dist):
    n = jnp.maximum(dist, 0)
    max_exact = N_BUCKETS // 2
    nf = jnp.maximum(n, 1).astype(F32)
    large = max_exact + (jnp.log(nf / max_exact) / math.log(MAX_DISTANCE / max_exact)
                         * (N_BUCKETS - max_exact)).astype(jnp.int32)
    large = jnp.minimum(large, N_BUCKETS - 1)
    return jnp.where(n < max_exact, n, large)


def _rel_bias_fn(rel_bias):
    far = rel_bias[N_BUCKETS - 1]

    def fn(dist):
        b = rel_bias[_t5_bucket(dist)] - far
        b = jnp.where((dist >= 0)[..., None], b, MASKED)
        return jnp.moveaxis(b, -1, 0)

    return fn


def _layer(xp2, xs2, batch, seq, db, ds, cache_k, cache_v, state_conv, page_table, rel_bias, lam, lam_init, p):
    d = xp2.shape[1]
    w = ATTN_WIDTH
    qwc = (jnp.tile(p["q_norm_w"], 2 * N_HEADS) * (HEAD_DIM ** -0.5)).reshape(w, 1)
    kw = jnp.tile(p["k_norm_w"], 2 * N_HEADS).reshape(1, w)
    gid = np.arange(w) // HEAD_DIM
    gmat = jnp.asarray((gid[:, None] == gid[None, :]).astype(np.float32) / HEAD_DIM, BF16)
    n1 = p["norm1_w"].reshape(1, d)
    win_b = p["w_in"].astype(BF16)
    lam1 = lam.reshape(1).astype(F32)
    bias_fn = _rel_bias_fn(rel_bias)

    qt_p, kb_p, k32_p, vt_p, v32_p, u_p = _proj(xp2, n1, win_b, qwc, kw, gmat)
    qt_s, _, k32_s, _, v32_s, u_s = _proj(xs2, n1, win_b, qwc, kw, gmat)

    cw = p["conv_w"]
    cb = p["conv_b"].reshape(1, CONV_DIM)
    lnw = p["conv_ln_w"].reshape(1, CONV_DIM)
    lnb = p["conv_ln_b"].reshape(1, CONV_DIM)
    oc_p = _pconv(u_p, batch, seq, cw, cb, lnw, lnb)
    oc_s, conv_s = _sconv(state_conv, u_s.reshape(db, ds, CONV_DIM), cw, cb, lnw, lnb)
    conv_p = u_p.reshape(batch, seq, CONV_DIM)[:, seq - (CONV_W - 1):]

    tq, tk = PATTN_TQ, KV_TILE
    kk = np.arange(tk)[:, None]
    qq = np.arange(tq)[None, :]
    dist_tiles = np.stack([(1 - c) * tk + qq - kk for c in range(1 + tq // tk)])
    bias_tiles = bias_fn(jnp.asarray(dist_tiles, jnp.int32))
    swc = p["subln_w"].reshape(V_DIM, 1)
    a_p = _pattn(lam1, qt_p, kb_p, vt_p, bias_tiles, swc, batch, seq, lam_init)

    past = page_table.shape[1] * PAGE_SIZE
    nrow = 2 * N_HEADS * ds
    row_head = np.arange(nrow) // (2 * ds)
    row_tok = np.arange(nrow) % ds
    col_head = np.arange(ROWS_PER_PAGE) % N_HEADS
    col_slot = np.arange(ROWS_PER_PAGE) // N_HEADS
    same_head = row_head[:, None] == col_head[None, :]
    hm = jnp.asarray(np.where(same_head, 0.0, MASKED), F32)
    dist_last = (past + row_tok)[:, None] - (past - PAGE_SIZE + col_slot)[None, :]
    b_last = bias_fn(jnp.asarray(dist_last, jnp.int32))
    blast = jnp.where(jnp.asarray(same_head), b_last[row_head, np.arange(nrow)], MASKED)
    ncol_head = np.arange(ds * N_HEADS) % N_HEADS
    ncol_tok = np.arange(ds * N_HEADS) // N_HEADS
    same_head_n = row_head[:, None] == ncol_head[None, :]
    b_new = bias_fn(jnp.asarray(row_tok[:, None] - ncol_tok[None, :], jnp.int32))
    bnew = jnp.where(jnp.asarray(same_head_n), b_new[row_head, np.arange(nrow)], MASKED)
    swr = p["subln_w"].reshape(1, V_DIM)
    q_s = (qt_s.T.astype(F32)).reshape(db, ds, w)
    kn3 = k32_s.reshape(db, ds * N_HEADS, V_DIM)
    vn3 = v32_s.reshape(db, ds * N_HEADS, V_DIM)
    a_s = _sattn(page_table, lam1, q_s, kn3, vn3, hm, blast, bnew, swr, cache_k, cache_v, lam_init)

    wo_b = p["w_out"].astype(BF16)
    n2 = p["norm2_w"].reshape(1, d)
    wr = jnp.concatenate([p["w_group"], p["w_expert"]], axis=1).T
    br = jnp.concatenate([p["b_group"], p["b_expert"]]).reshape(-1, 1)
    tri = jnp.asarray(np.triu(np.ones((MIX_TM, MIX_TM), np.float32), 1), BF16)
    cnt0 = jnp.zeros((N_EXPERTS, 1), F32)
    x2_p, h_p, route_p, cnt_p = _mix(a_p, oc_p, xp2, wo_b, n2, wr, br, tri, cnt0)
    x2_s, h_s, route_s, cnt = _mix(a_s.reshape(db * ds, w), oc_s.reshape(db * ds, CONV_DIM), xs2,
                                   wo_b, n2, wr, br, tri, cnt_p)

    bm = EXPERT_BM
    counts = cnt[:, 0].astype(jnp.int32)
    padded = (counts + bm - 1) // bm * bm
    pad_end = jnp.cumsum(padded)
    pad_start = pad_end - padded
    n_assign = 2 * (xp2.shape[0] + xs2.shape[0])
    n_blocks = -(-n_assign // bm) + N_EXPERTS
    nslot = n_blocks * bm
    block_expert = jnp.minimum(
        jnp.searchsorted(pad_end, jnp.arange(n_blocks, dtype=jnp.int32) * bm, side="right"),
        N_EXPERTS - 1).astype(jnp.int32)
    n_used = (pad_end[-1] // bm).reshape(1).astype(jnp.int32)

    def slots_of(route):
        e = route[0:2].astype(jnp.int32)
        rank = route[2:4].astype(jnp.int32)
        s = pad_start[e] + rank
        t = s.shape[1]
        return s.reshape(2, t // ROW_TM, ROW_TM).transpose(1, 0, 2)

    slots_p, slots_s = slots_of(route_p), slots_of(route_s)

    xb = jnp.zeros((nslot, d), F32)
    xb = _dispatch(slots_p, h_p, xb)
    xb = _dispatch(slots_s, h_s, xb)
    yb = _experts(block_expert, n_used, xb, p["w1"].astype(BF16), p["w3"].astype(BF16), p["w2"].astype(BF16))
    y_p = _combine(slots_p, x2_p, route_p[4:6].T, yb)
    y_s = _combine(slots_s, x2_s, route_s[4:6].T, yb)

    kshape = (N_HEADS, V_DIM)
    return (y_p, y_s,
            k32_p.reshape(batch, seq, *kshape), v32_p.reshape(batch, seq, *kshape), conv_p,
            k32_s.reshape(db, ds, *kshape), v32_s.reshape(db, ds, *kshape), conv_s)


def kernel(x_prompt, x_sample, cache_k, cache_v, state_conv, page_table, rel_bias, norm1_w, w_in, q_norm_w, k_norm_w, lambda_q1, lambda_k1, lambda_q2, lambda_k2, subln_w, conv_w, conv_b, conv_ln_w, conv_ln_b, w_out, norm2_w, w_group, b_group, w_expert, b_expert, w1, w3, w2):
    batch, seq, d = x_prompt.shape
    db, ds, _ = x_sample.shape
    depth = w_in.shape[0]
    assert depth == 1, "paged caches are read once; a deeper trunk needs one cache slice per layer"
    xp = x_prompt.reshape(batch * seq, d)
    xs = x_sample.reshape(db * ds, d)
    l = 0
    lam_init = 0.8 - 0.6 * math.exp(-0.3 * l)
    lam = (jnp.exp(jnp.sum(lambda_q1[l].astype(F32) * lambda_k1[l].astype(F32)))
           - jnp.exp(jnp.sum(lambda_q2[l].astype(F32) * lambda_k2[l].astype(F32)))
           + lam_init)
    params = dict(norm1_w=norm1_w[l], w_in=w_in[l], q_norm_w=q_norm_w[l], k_norm_w=k_norm_w[l],
                  subln_w=subln_w[l], conv_w=conv_w[l], conv_b=conv_b[l], conv_ln_w=conv_ln_w[l],
                  conv_ln_b=conv_ln_b[l], w_out=w_out[l], norm2_w=norm2_w[l], w_group=w_group[l],
                  b_group=b_group[l], w_expert=w_expert[l], b_expert=b_expert[l],
                  w1=w1[l], w3=w3[l], w2=w2[l])
    y_p, y_s, k_p, v_p, c_p, k_s, v_s, c_s = _layer(
        xp, xs, batch, seq, db, ds, cache_k, cache_v, state_conv[l], page_table, rel_bias, lam, lam_init, params)
    return (y_p.reshape(batch, seq, d), y_s.reshape(db, ds, d),
            k_p[None], v_p[None], c_p[None], k_s[None], v_s[None], c_s[None])
```

```python
import functools
import math

import numpy as np
import jax
import jax.numpy as jnp
from jax import lax
from jax.experimental import pallas as pl
from jax.experimental.pallas import tpu as pltpu

F32 = jnp.float32
BF16 = jnp.bfloat16

N_HEADS = 4
HEAD_DIM = 64
V_DIM = 2 * HEAD_DIM
ATTN_WIDTH = N_HEADS * V_DIM
CONV_DIM = 512
CONV_W = 31
N_BUCKETS = 32
MAX_DISTANCE = 128
N_GROUPS = 8
EXPERTS_PER_GROUP = 8
N_EXPERTS = N_GROUPS * EXPERTS_PER_GROUP
PAGE_SIZE = 128
EPS = 1e-6
MASKED = -1e30

VMEM_LIMIT = 56 * 1024 * 1024


def _cparams(sem):
    return pltpu.CompilerParams(dimension_semantics=sem, vmem_limit_bytes=VMEM_LIMIT)


PROJ_TM = 512
KV_TILE = 512
LOG2E = math.log2(math.e)


def _proj_kernel(x_ref, n1_ref, win_ref, qwc_ref, kw_ref, g_ref,
                 qt_ref, kb_ref, k32_ref, vt_ref, v32_ref, u_ref):
    tm = x_ref.shape[0]
    x = x_ref[...]
    ms = jnp.mean(x * x, axis=-1, keepdims=True)
    xn = (x * lax.rsqrt(ms + EPS)) * n1_ref[...]
    z = jnp.dot(xn.astype(BF16), win_ref[...], preferred_element_type=F32)
    w = ATTN_WIDTH
    zq, zk, zv = z[:, :w], z[:, w:2 * w], z[:, 2 * w:3 * w]
    g1, g2 = z[:, 3 * w:3 * w + CONV_DIM], z[:, 3 * w + CONV_DIM:]

    kk = zk * zk
    hi = kk.astype(BF16)
    lo = (kk - hi.astype(F32)).astype(BF16)
    kms = (jnp.dot(hi, g_ref[...], preferred_element_type=F32)
           + jnp.dot(lo, g_ref[...], preferred_element_type=F32))
    kn = (zk * lax.rsqrt(kms + EPS)) * kw_ref[...]
    k32_ref[...] = kn
    kb_ref[...] = kn.astype(BF16)

    v32_ref[...] = zv
    zvt = zv.T
    for t in range(tm // KV_TILE):
        vt_ref[t] = zvt[:, t * KV_TILE:(t + 1) * KV_TILE].astype(BF16)

    q3 = zq.T.reshape(2 * N_HEADS, HEAD_DIM, tm)
    qms = jnp.mean(q3 * q3, axis=1, keepdims=True)
    qn = (q3 * lax.rsqrt(qms + EPS)).reshape(w, tm) * qwc_ref[...]
    qt_ref[...] = qn.astype(BF16)

    u_ref[...] = g1 * (1.0 / (1.0 + jnp.exp(-g2)))


def _proj(x2d, n1, win_b, qwc, kw, gmat):
    t = x2d.shape[0]
    tm = PROJ_TM
    assert t % tm == 0
    w = ATTN_WIDTH
    const = lambda shape: pl.BlockSpec(shape, lambda i: (0,) * len(shape))
    return pl.pallas_call(
        _proj_kernel,
        grid=(t // tm,),
        in_specs=[
            pl.BlockSpec((tm, x2d.shape[1]), lambda i: (i, 0)),
            const(n1.shape), const(win_b.shape), const(qwc.shape), const(kw.shape), const(gmat.shape),
        ],
        out_specs=[
            pl.BlockSpec((w, tm), lambda i: (0, i)),
            pl.BlockSpec((tm, w), lambda i: (i, 0)),
            pl.BlockSpec((tm, w), lambda i: (i, 0)),
            pl.BlockSpec((tm // KV_TILE, w, KV_TILE), lambda i: (i, 0, 0)),
            pl.BlockSpec((tm, w), lambda i: (i, 0)),
            pl.BlockSpec((tm, CONV_DIM), lambda i: (i, 0)),
        ],
        out_shape=[
            jax.ShapeDtypeStruct((w, t), BF16),
            jax.ShapeDtypeStruct((t, w), BF16),
            jax.ShapeDtypeStruct((t, w), F32),
            jax.ShapeDtypeStruct((t // KV_TILE, w, KV_TILE), BF16),
            jax.ShapeDtypeStruct((t, w), F32),
            jax.ShapeDtypeStruct((t, CONV_DIM), F32),
        ],
        compiler_params=_cparams(("arbitrary",)),
        name="proj",
    )(x2d, n1, win_b, qwc, kw, gmat)


CONV_TC = 512
CONV_RC = 64
CONV_HALO = 32


def _ln_swish(c, lnw, lnb):
    mu = jnp.mean(c, axis=-1, keepdims=True)
    d = c - mu
    var = jnp.mean(d * d, axis=-1, keepdims=True)
    cn = (d * lax.rsqrt(var + EPS)) * lnw + lnb
    return cn * (1.0 / (1.0 + jnp.exp(-cn)))


def _pconv_kernel(prev_ref, cur_ref, w_ref, b_ref, lnw_ref, lnb_ref, o_ref, win_ref):
    i = pl.program_id(1)
    tc = cur_ref.shape[0]
    n = CONV_HALO + tc
    win_ref[0, 0:CONV_HALO, :] = jnp.where(i > 0, prev_ref[...], 0.0)
    win_ref[0, CONV_HALO:n, :] = cur_ref[...]
    win_ref[0, n:n + 8, :] = jnp.zeros((8, CONV_DIM), F32)
    step = n // 4
    for r in range(1, 8):
        for c0 in range(0, n, step):
            win_ref[r, c0:c0 + step, :] = win_ref[0, c0 + r:c0 + r + step, :]
    off = CONV_HALO - (CONV_W - 1)

    def chunk(c, carry):
        base = pl.multiple_of(c * CONV_RC, CONV_RC)
        acc = jnp.zeros((CONV_RC, CONV_DIM), F32) + b_ref[...]
        for k in range(CONV_W):
            shift = k + off
            rows = pl.ds(pl.multiple_of(base + (shift // 8) * 8, 8), CONV_RC)
            acc = acc + w_ref[k:k + 1, :] * win_ref[shift % 8, rows, :]
        o_ref[pl.ds(base, CONV_RC), :] = _ln_swish(acc, lnw_ref[...], lnb_ref[...]).astype(o_ref.dtype)
        return carry

    lax.fori_loop(0, tc // CONV_RC, chunk, 0)


def _pconv(u2d, batch, seq, cw, cb, lnw, lnb):
    tc = CONV_TC
    assert seq % tc == 0
    nt = seq // tc
    const = lambda shape: pl.BlockSpec(shape, lambda b, i: (0,) * len(shape))
    return pl.pallas_call(
        _pconv_kernel,
        grid=(batch, nt),
        in_specs=[
            pl.BlockSpec((CONV_HALO, CONV_DIM),
                         lambda b, i: (jnp.maximum((b * seq + i * tc) // CONV_HALO - 1, 0), 0)),
            pl.BlockSpec((tc, CONV_DIM), lambda b, i: (b * nt + i, 0)),
            const(cw.shape), const(cb.shape), const(lnw.shape), const(lnb.shape),
        ],
        out_specs=pl.BlockSpec((tc, CONV_DIM), lambda b, i: (b * nt + i, 0)),
        out_shape=jax.ShapeDtypeStruct((batch * seq, CONV_DIM), BF16),
        scratch_shapes=[pltpu.VMEM((8, CONV_HALO + tc + 8, CONV_DIM), F32)],
        compiler_params=_cparams(("arbitrary", "arbitrary")),
        name="pconv",
    )(u2d, u2d, cw, cb, lnw, lnb)


SCONV_R = 8


def _sconv_kernel(st_ref, u_ref, w_ref, b_ref, lnw_ref, lnb_ref, o_ref, ns_ref, win_ref):
    nprev = st_ref.shape[1]
    ds = u_ref.shape[1]
    for r in range(st_ref.shape[0]):
        win_ref[0:nprev, :] = st_ref[r]
        win_ref[nprev:nprev + ds, :] = u_ref[r]
        acc = jnp.zeros((ds, CONV_DIM), F32) + b_ref[...]
        for k in range(CONV_W):
            acc = acc + w_ref[k:k + 1, :] * win_ref[k:k + ds, :]
        o_ref[r] = _ln_swish(acc, lnw_ref[...], lnb_ref[...])
        ns_ref[r] = win_ref[ds:ds + nprev, :]


def _sconv(state, u3, cw, cb, lnw, lnb):
    db, nprev, _ = state.shape
    ds = u3.shape[1]
    r = SCONV_R
    assert db % r == 0 and nprev == CONV_W - 1
    const = lambda shape: pl.BlockSpec(shape, lambda i: (0,) * len(shape))
    return pl.pallas_call(
        _sconv_kernel,
        grid=(db // r,),
        in_specs=[
            pl.BlockSpec((r, nprev, CONV_DIM), lambda i: (i, 0, 0)),
            pl.BlockSpec((r, ds, CONV_DIM), lambda i: (i, 0, 0)),
            const(cw.shape), const(cb.shape), const(lnw.shape), const(lnb.shape),
        ],
        out_specs=[
            pl.BlockSpec((r, ds, CONV_DIM), lambda i: (i, 0, 0)),
            pl.BlockSpec((r, nprev, CONV_DIM), lambda i: (i, 0, 0)),
        ],
        out_shape=[
            jax.ShapeDtypeStruct((db, ds, CONV_DIM), F32),
            jax.ShapeDtypeStruct((db, nprev, CONV_DIM), F32),
        ],
        scratch_shapes=[pltpu.VMEM((nprev + ds + 2, CONV_DIM), F32)],
        compiler_params=_cparams(("arbitrary",)),
        name="sconv",
    )(state, u3, cw, cb, lnw, lnb)


PATTN_TQ = KV_TILE


def _pattn_kernel(lam_ref, qt_ref, kb_ref, vt_ref, bias_ref, sw_ref, o_ref, m_ref, l_ref, acc_ref, s_ref,
                  *, lam_init):
    i = pl.program_id(2)
    tk = KV_TILE
    m_ref[...] = jnp.full(m_ref.shape, -jnp.inf, F32)
    l_ref[...] = jnp.zeros(l_ref.shape, F32)
    acc_ref[...] = jnp.zeros(acc_ref.shape, F32)

    def scores(c, buf):
        k0 = pl.multiple_of(c * tk, tk)
        for mm in range(2):
            kt = kb_ref[pl.ds(k0, tk), mm * HEAD_DIM:(mm + 1) * HEAD_DIM]
            s_ref[buf, mm] = jnp.dot(kt, qt_ref[mm * HEAD_DIM:(mm + 1) * HEAD_DIM, :],
                                     preferred_element_type=F32)

    def consume(c, buf, bias):
        ps, alphas = [], []
        for mm in range(2):
            s = s_ref[buf, mm]
            if bias is not None:
                s = s + bias
            m_prev = m_ref[mm]
            m_new = jnp.maximum(m_prev, jnp.max(s, axis=0, keepdims=True))
            alpha = jnp.exp2(m_prev - m_new)
            p = jnp.exp2(s - m_new)
            l_ref[mm] = alpha * l_ref[mm] + jnp.sum(p, axis=0, keepdims=True)
            m_ref[mm] = m_new
            ps.append(p.astype(BF16))
            alphas.append(alpha)
        for mm in range(2):
            pv = jnp.dot(vt_ref[c], ps[mm], preferred_element_type=F32)
            acc_ref[mm] = alphas[mm] * acc_ref[mm] + pv

    scores(0, 0)

    def far_pair(t, carry):
        scores(2 * t + 1, 1)
        consume(2 * t, 0, None)
        scores(2 * t + 2, 0)
        consume(2 * t + 1, 1, None)
        return carry

    lax.fori_loop(0, jnp.maximum(i - 1, 0) // 2, far_pair, 0)

    @pl.when(i == 0)
    def _():
        consume(0, 0, bias_ref[1])

    @pl.when(i % 2 == 1)
    def _():
        scores(i, 1)
        consume(i - 1, 0, bias_ref[0])
        consume(i, 1, bias_ref[1])

    @pl.when((i % 2 == 0) & (i > 0))
    def _():
        scores(i - 1, 1)
        consume(i - 2, 0, None)
        scores(i, 0)
        consume(i - 1, 1, bias_ref[0])
        consume(i, 0, bias_ref[1])

    lam = lam_ref[0]
    o = acc_ref[0] / l_ref[0] - lam * (acc_ref[1] / l_ref[1])
    ms = jnp.mean(o * o, axis=0, keepdims=True)
    on = ((o * lax.rsqrt(ms + EPS)) * sw_ref[...]) * (1.0 - lam_init)
    o_ref[...] = on.T.astype(o_ref.dtype)


def _pattn(lam, qt, kb, vt, bias_tiles, swc, batch, seq, lam_init):
    tq = PATTN_TQ
    tk = KV_TILE
    assert seq % tq == 0 and tq == tk
    nq = seq // tq
    nkv = seq // tk
    return pl.pallas_call(
        functools.partial(_pattn_kernel, lam_init=lam_init),
        grid=(batch, N_HEADS, nq),
        in_specs=[
            pl.BlockSpec(memory_space=pltpu.SMEM),
            pl.BlockSpec((V_DIM, tq), lambda b, h, i: (h, b * nq + i)),
            pl.BlockSpec((seq, V_DIM), lambda b, h, i: (b, h)),
            pl.BlockSpec((nkv, V_DIM, tk), lambda b, h, i: (b, h, 0)),
            pl.BlockSpec((None, 1 + tq // tk, tk, tq), lambda b, h, i: (h, 0, 0, 0)),
            pl.BlockSpec(swc.shape, lambda b, h, i: (0, 0)),
        ],
        out_specs=pl.BlockSpec((tq, V_DIM), lambda b, h, i: (b * nq + i, h)),
        out_shape=jax.ShapeDtypeStruct((batch * seq, ATTN_WIDTH), BF16),
        scratch_shapes=[
            pltpu.VMEM((2, 1, tq), F32),
            pltpu.VMEM((2, 1, tq), F32),
            pltpu.VMEM((2, V_DIM, tq), F32),
            pltpu.VMEM((2, 2, tk, tq), F32),
        ],
        compiler_params=_cparams(("arbitrary", "arbitrary", "arbitrary")),
        name="pattn",
    )(lam, qt, kb, vt, bias_tiles, swc)


SATTN_CP = 8
SATTN_NBUF = 4
ROWS_PER_PAGE = PAGE_SIZE * N_HEADS


def _sattn_kernel(pt_ref, lam_ref, q_ref, kn_ref, vn_ref, hm_ref, blast_ref, bnew_ref, sw_ref,
                  ck_ref, cv_ref, o_ref, kbuf, vbuf, ksem, vsem, *, n_req, n_chunks, lam_init):
    r = pl.program_id(0)
    cp = SATTN_CP
    nbuf = SATTN_NBUF
    rows = cp * ROWS_PER_PAGE

    def page_copies(req, chunk, slot):
        out = []
        for p in range(cp):
            pg = pt_ref[req, chunk * cp + p]
            out.append(pltpu.make_async_copy(ck_ref.at[0, pg], kbuf.at[slot, p], ksem.at[slot]))
            out.append(pltpu.make_async_copy(cv_ref.at[0, pg], vbuf.at[slot, p], vsem.at[slot]))
        return out

    def start_chunk(g):
        req = g // n_chunks
        chunk = g % n_chunks
        for c in page_copies(req, chunk, g % nbuf):
            c.start()

    @pl.when(r == 0)
    def _():
        for g in range(nbuf - 1):
            start_chunk(g)

    qs = q_ref[0]
    ds = qs.shape[0]
    lane = lax.broadcasted_iota(jnp.int32, (ds, V_DIM), 1)
    pieces = []
    for h in range(N_HEADS):
        qh = qs[:, h * V_DIM:(h + 1) * V_DIM]
        for mm in range(2):
            keep = (lane < HEAD_DIM) if mm == 0 else (lane >= HEAD_DIM)
            pieces.append(jnp.where(keep, qh, 0.0))
    q2 = jnp.concatenate(pieces, axis=0).astype(BF16)
    nrow = q2.shape[0]

    def step(s, state, v2):
        m_prev, l_prev, acc = state
        m_new = jnp.maximum(m_prev, jnp.max(s, axis=1, keepdims=True))
        alpha = jnp.exp2(m_prev - m_new)
        p = jnp.exp2(s - m_new)
        l_new = alpha * l_prev + jnp.sum(p, axis=1, keepdims=True)
        acc_new = alpha * acc + jnp.dot(p.astype(BF16), v2, preferred_element_type=F32)
        return m_new, l_new, acc_new

    nt = (((1,), (1,)), ((), ()))
    hm = hm_ref[...]
    hm_chunk = jnp.concatenate([hm] * cp, axis=1)
    last_mask = jnp.concatenate([hm] * (cp - 1) + [blast_ref[...]], axis=1)

    def chunk_body(c, state):
        g = r * n_chunks + c
        nxt = g + (nbuf - 1)

        @pl.when(nxt < n_req * n_chunks)
        def _():
            start_chunk(nxt)

        slot = g % nbuf
        for cpy in page_copies(r, c, slot):
            cpy.wait()
        k2 = kbuf[slot].reshape(rows, V_DIM).astype(BF16)
        v2 = vbuf[slot].reshape(rows, V_DIM).astype(BF16)
        s = lax.dot_general(q2, k2, nt, preferred_element_type=F32)
        s = s + jnp.where(c == n_chunks - 1, last_mask, hm_chunk)
        return step(s, state, v2)

    state = (jnp.full((nrow, 1), -jnp.inf, F32), jnp.zeros((nrow, 1), F32), jnp.zeros((nrow, V_DIM), F32))
    state = lax.fori_loop(0, n_chunks, chunk_body, state)

    kn = kn_ref[0].astype(BF16)
    vn = vn_ref[0].astype(BF16)
    s = lax.dot_general(q2, kn, nt, preferred_element_type=F32) + bnew_ref[...]
    _, l_fin, acc = step(s, state, vn)

    lam = lam_ref[0]
    o = acc / l_fin
    outs = []
    for h in range(N_HEADS):
        o1 = o[(2 * h) * ds:(2 * h + 1) * ds]
        o2 = o[(2 * h + 1) * ds:(2 * h + 2) * ds]
        oh = o1 - lam * o2
        ms = jnp.mean(oh * oh, axis=-1, keepdims=True)
        outs.append(((oh * lax.rsqrt(ms + EPS)) * sw_ref[...]) * (1.0 - lam_init))
    o_ref[0] = jnp.concatenate(outs, axis=1)


def _sattn(page_table, lam, q3, kn3, vn3, hm, blast, bnew, swr, cache_k, cache_v, lam_init):
    n_req, ds, _ = q3.shape
    n_pages = page_table.shape[1]
    assert n_pages % SATTN_CP == 0
    n_chunks = n_pages // SATTN_CP
    assert n_chunks % SATTN_NBUF == 0 or SATTN_NBUF % n_chunks == 0 or True
    nrow = 2 * N_HEADS * ds
    rows = SATTN_CP * ROWS_PER_PAGE
    kernel = functools.partial(_sattn_kernel, n_req=n_req, n_chunks=n_chunks, lam_init=lam_init)
    const = lambda shape: pl.BlockSpec(shape, lambda i, pt: (0,) * len(shape))
    return pl.pallas_call(
        kernel,
        grid_spec=pltpu.PrefetchScalarGridSpec(
            num_scalar_prefetch=1,
            grid=(n_req,),
            in_specs=[
                pl.BlockSpec(memory_space=pltpu.SMEM),
                pl.BlockSpec((1, ds, ATTN_WIDTH), lambda i, pt: (i, 0, 0)),
                pl.BlockSpec((1, ds * N_HEADS, V_DIM), lambda i, pt: (i, 0, 0)),
                pl.BlockSpec((1, ds * N_HEADS, V_DIM), lambda i, pt: (i, 0, 0)),
                const(hm.shape), const(blast.shape), const(bnew.shape), const(swr.shape),
                pl.BlockSpec(memory_space=pl.ANY),
                pl.BlockSpec(memory_space=pl.ANY),
            ],
            out_specs=pl.BlockSpec((1, ds, ATTN_WIDTH), lambda i, pt: (i, 0, 0)),
            scratch_shapes=[
                pltpu.VMEM((SATTN_NBUF, SATTN_CP, PAGE_SIZE, N_HEADS, V_DIM), F32),
                pltpu.VMEM((SATTN_NBUF, SATTN_CP, PAGE_SIZE, N_HEADS, V_DIM), F32),
                pltpu.SemaphoreType.DMA((SATTN_NBUF,)),
                pltpu.SemaphoreType.DMA((SATTN_NBUF,)),
            ],
        ),
        out_shape=jax.ShapeDtypeStruct((n_req, ds, ATTN_WIDTH), F32),
        compiler_params=_cparams(("arbitrary",)),
        name="sattn",
    )(page_table, lam, q3, kn3, vn3, hm, blast, bnew, swr, cache_k, cache_v)


MIX_TM = 512
ROUTE_ROWS = 8


def _mix_kernel(a_ref, c_ref, x_ref, wo_ref, n2_ref, wr_ref, br_ref, tri_ref, cin_ref,
                x2_ref, h_ref, route_ref, cnt_ref):
    i = pl.program_id(0)
    tm = x_ref.shape[0]
    half = a_ref.shape[1]

    @pl.when(i == 0)
    def _():
        cnt_ref[...] = cin_ref[...]

    x2 = (x_ref[...]
          + jnp.dot(a_ref[...].astype(BF16), wo_ref[0:half, :], preferred_element_type=F32)
          + jnp.dot(c_ref[...].astype(BF16), wo_ref[half:, :], preferred_element_type=F32))
    x2_ref[...] = x2
    ms = jnp.mean(x2 * x2, axis=-1, keepdims=True)
    h = (x2 * lax.rsqrt(ms + EPS)) * n2_ref[...]
    h_ref[...] = h

    h_hi = h.astype(BF16)
    h_lo = (h - h_hi.astype(F32)).astype(BF16)
    wr = wr_ref[...]
    w_hi = wr.astype(BF16)
    w_lo = (wr - w_hi.astype(F32)).astype(BF16)
    nt = (((1,), (1,)), ((), ()))
    logits = (lax.dot_general(w_hi, h_hi, nt, preferred_element_type=F32)
              + lax.dot_general(w_hi, h_lo, nt, preferred_element_type=F32)
              + lax.dot_general(w_lo, h_hi, nt, preferred_element_type=F32)) + br_ref[...]
    gl = logits[0:N_GROUPS]
    el = logits[N_GROUPS:N_GROUPS + N_EXPERTS]

    sub8 = lax.broadcasted_iota(jnp.int32, (N_GROUPS, tm), 0)

    def first_argmax(vals):
        top = jnp.max(vals, axis=0, keepdims=True)
        idx = jnp.min(jnp.where(vals == top, sub8, N_GROUPS), axis=0, keepdims=True)
        return top, idx

    gexp = jnp.exp(gl - jnp.max(gl, axis=0, keepdims=True))
    gp = gexp / jnp.sum(gexp, axis=0, keepdims=True)
    g_prob, g_idx = first_argmax(gp)

    el3 = el.reshape(N_GROUPS, EXPERTS_PER_GROUP, tm)
    sel = jnp.zeros((EXPERTS_PER_GROUP, tm), F32)
    for g in range(N_GROUPS):
        sel = sel + jnp.where(g_idx == g, el3[g], 0.0)
    v1, i1 = first_argmax(sel)
    v2, i2 = first_argmax(jnp.where(sub8 == i1, -jnp.inf, sel))
    e2 = jnp.exp(v2 - v1)
    p1 = 1.0 / (1.0 + e2)
    p2 = e2 / (1.0 + e2)
    ea = g_idx * EXPERTS_PER_GROUP + i1
    eb = g_idx * EXPERTS_PER_GROUP + i2

    sub64 = lax.broadcasted_iota(jnp.int32, (N_EXPERTS, tm), 0)
    oh_a = sub64 == ea
    oh_b = sub64 == eb
    oh = jnp.where(oh_a | oh_b, 1.0, 0.0)
    before = jnp.dot(oh.astype(BF16), tri_ref[...], preferred_element_type=F32) + cnt_ref[...]
    rank_a = jnp.sum(jnp.where(oh_a, before, 0.0), axis=0, keepdims=True)
    rank_b = jnp.sum(jnp.where(oh_b, before, 0.0), axis=0, keepdims=True)
    cnt_ref[...] = cnt_ref[...] + jnp.sum(oh, axis=1, keepdims=True)

    route_ref[...] = jnp.concatenate(
        [ea.astype(F32), eb.astype(F32), rank_a, rank_b, g_prob * p1, g_prob * p2,
         jnp.zeros((ROUTE_ROWS - 6, tm), F32)], axis=0)


def _mix(a, c, x2d, wo_b, n2, wr, br, tri, cnt_in):
    t, d = x2d.shape
    tm = MIX_TM
    assert t % tm == 0
    const = lambda shape: pl.BlockSpec(shape, lambda i: (0,) * len(shape))
    return pl.pallas_call(
        _mix_kernel,
        grid=(t // tm,),
        in_specs=[
            pl.BlockSpec((tm, a.shape[1]), lambda i: (i, 0)),
            pl.BlockSpec((tm, c.shape[1]), lambda i: (i, 0)),
            pl.BlockSpec((tm, d), lambda i: (i, 0)),
            const(wo_b.shape), const(n2.shape), const(wr.shape), const(br.shape), const(tri.shape),
            const(cnt_in.shape),
        ],
        out_specs=[
            pl.BlockSpec((tm, d), lambda i: (i, 0)),
            pl.BlockSpec((tm, d), lambda i: (i, 0)),
            pl.BlockSpec((ROUTE_ROWS, tm), lambda i: (0, i)),
            const(cnt_in.shape),
        ],
        out_shape=[
            jax.ShapeDtypeStruct((t, d), F32),
            jax.ShapeDtypeStruct((t, d), F32),
            jax.ShapeDtypeStruct((ROUTE_ROWS, t), F32),
            jax.ShapeDtypeStruct(cnt_in.shape, F32),
        ],
        compiler_params=_cparams(("arbitrary",)),
        name="mix",
    )(a, c, x2d, wo_b, n2, wr, br, tri, cnt_in)


ROW_TM = 256
ROW_UNROLL = 8
EXPERT_BM = 256


def _dispatch_kernel(slots_ref, h_ref, xin_ref, xb_ref, sem):
    del xin_ref
    tm = h_ref.shape[0]

    def issue(t, carry):
        for j in range(2):
            s = slots_ref[0, j, t]
            pltpu.make_async_copy(h_ref.at[pl.ds(t, 1)], xb_ref.at[pl.ds(s, 1)], sem).start(priority=j)
        return carry

    lax.fori_loop(0, tm, issue, 0, unroll=ROW_UNROLL)

    def drain(t, carry):
        for j in range(2):
            pltpu.make_async_copy(h_ref.at[pl.ds(0, 1)], xb_ref.at[pl.ds(0, 1)], sem).wait()
        return carry

    lax.fori_loop(0, tm, drain, 0, unroll=ROW_UNROLL)


def _dispatch(slots3, h, xb):
    t, d = h.shape
    tm = ROW_TM
    assert t % tm == 0
    return pl.pallas_call(
        _dispatch_kernel,
        grid=(t // tm,),
        in_specs=[
            pl.BlockSpec((1, 2, tm), lambda i: (i, 0, 0), memory_space=pltpu.SMEM),
            pl.BlockSpec((tm, d), lambda i: (i, 0)),
            pl.BlockSpec(memory_space=pl.ANY),
        ],
        out_specs=pl.BlockSpec(memory_space=pl.ANY),
        out_shape=jax.ShapeDtypeStruct(xb.shape, xb.dtype),
        scratch_shapes=[pltpu.SemaphoreType.DMA],
        input_output_aliases={2: 0},
        compiler_params=_cparams(("arbitrary",)),
        name="dispatch",
    )(slots3, h, xb)


def _experts_kernel(be_ref, nb_ref, xb_ref, w1_ref, w3_ref, w2_ref, yb_ref):
    i = pl.program_id(0)

    @pl.when(i < nb_ref[0])
    def _():
        x = xb_ref[...].astype(BF16)
        a = jnp.dot(x, w1_ref[0], preferred_element_type=F32)
        b = jnp.dot(x, w3_ref[0], preferred_element_type=F32)
        mid = (a * (1.0 / (1.0 + jnp.exp(-a)))) * b
        yb_ref[...] = jnp.dot(mid.astype(BF16), w2_ref[0], preferred_element_type=F32)

    @pl.when(i >= nb_ref[0])
    def _():
        yb_ref[...] = jnp.zeros(yb_ref.shape, F32)


def _experts(block_expert, n_used, xb, w1b, w3b, w2b):
    nslot, d = xb.shape
    bm = EXPERT_BM
    hid = w1b.shape[2]
    return pl.pallas_call(
        _experts_kernel,
        grid_spec=pltpu.PrefetchScalarGridSpec(
            num_scalar_prefetch=2,
            grid=(nslot // bm,),
            in_specs=[
                pl.BlockSpec((bm, d), lambda i, be, nb: (i, 0)),
                pl.BlockSpec((1, d, hid), lambda i, be, nb: (be[i], 0, 0)),
                pl.BlockSpec((1, d, hid), lambda i, be, nb: (be[i], 0, 0)),
                pl.BlockSpec((1, hid, d), lambda i, be, nb: (be[i], 0, 0)),
            ],
            out_specs=pl.BlockSpec((bm, d), lambda i, be, nb: (i, 0)),
        ),
        out_shape=jax.ShapeDtypeStruct((nslot, d), F32),
        compiler_params=_cparams(("arbitrary",)),
        name="experts",
    )(block_expert, n_used, xb, w1b, w3b, w2b)


def _combine_kernel(slots_ref, x2_ref, g_ref, yb_ref, o_ref, ybuf, sem):
    tm = x2_ref.shape[0]

    def issue(t, carry):
        for j in range(2):
            s = slots_ref[0, j, t]
            pltpu.make_async_copy(yb_ref.at[pl.ds(s, 1)], ybuf.at[j, pl.ds(t, 1)], sem).start(priority=j)
        return carry

    lax.fori_loop(0, tm, issue, 0, unroll=ROW_UNROLL)

    def drain(t, carry):
        for j in range(2):
            pltpu.make_async_copy(yb_ref.at[pl.ds(0, 1)], ybuf.at[j, pl.ds(0, 1)], sem).wait()
        return carry

    lax.fori_loop(0, tm, drain, 0, unroll=ROW_UNROLL)
    g = g_ref[...]
    o_ref[...] = x2_ref[...] + (g[:, 0:1] * ybuf[0] + g[:, 1:2] * ybuf[1])


def _combine(slots3, x2, gates, yb):
    t, d = x2.shape
    tm = ROW_TM
    assert t % tm == 0
    return pl.pallas_call(
        _combine_kernel,
        grid=(t // tm,),
        in_specs=[
            pl.BlockSpec((1, 2, tm), lambda i: (i, 0, 0), memory_space=pltpu.SMEM),
            pl.BlockSpec((tm, d), lambda i: (i, 0)),
            pl.BlockSpec((tm, 2), lambda i: (i, 0)),
            pl.BlockSpec(memory_space=pl.ANY),
        ],
        out_specs=pl.BlockSpec((tm, d), lambda i: (i, 0)),
        out_shape=jax.ShapeDtypeStruct((t, d), F32),
        scratch_shapes=[pltpu.VMEM((2, tm, d), F32), pltpu.SemaphoreType.DMA],
        compiler_params=_cparams(("arbitrary",)),
        name="combine",
    )(slots3, x2, gates, yb)


def _t5_bucket(dist):
    n = np.maximum(dist, 0)
    max_exact = N_BUCKETS // 2
    nf = np.maximum(n, 1).astype(np.float32)
    large = max_exact + (np.log(nf / np.float32(max_exact)) / np.float32(math.log(MAX_DISTANCE / max_exact))
                         * np.float32(N_BUCKETS - max_exact)).astype(np.int32)
    large = np.minimum(large, N_BUCKETS - 1)
    return np.where(n < max_exact, n, large)


def _bias_table(rel_bias, d_lo, d_hi):
    dist = np.arange(d_lo, d_hi)
    onehot = (_t5_bucket(dist)[None, :] == np.arange(N_BUCKETS)[:, None]).astype(np.float32)
    tab = jnp.sum(rel_bias.T[:, :, None] * onehot[None], axis=1)
    tab = (tab - rel_bias[N_BUCKETS - 1][:, None]) * LOG2E
    return jnp.where(jnp.asarray(dist >= 0)[None, :], tab, MASKED)


def _toeplitz(v, nr, nc):
    n = nr + nc - 1
    assert v.shape[1] == n
    w = jnp.concatenate([v, jnp.zeros((v.shape[0], 1), v.dtype)], axis=1)
    flat = jnp.tile(w, (1, nr))[:, :nr * n]
    return flat.reshape(v.shape[0], nr, n)[:, :, nr - 1:nr - 1 + nc]


def _layer(xp2, xs2, batch, seq, db, ds, cache_k, cache_v, state_conv, page_table, rel_bias, lam, lam_init, p):
    d = xp2.shape[1]
    w = ATTN_WIDTH
    qwc = (jnp.tile(p["q_norm_w"], 2 * N_HEADS) * (HEAD_DIM ** -0.5 * LOG2E)).reshape(w, 1)
    kw = jnp.tile(p["k_norm_w"], 2 * N_HEADS).reshape(1, w)
    gid = np.arange(w) // HEAD_DIM
    gmat = jnp.asarray((gid[:, None] == gid[None, :]).astype(np.float32) / HEAD_DIM, BF16)
    n1 = p["norm1_w"].reshape(1, d)
    win_b = p["w_in"].astype(BF16)
    lam1 = lam.reshape(1).astype(F32)

    qt_p, kb_p, k32_p, vt_p, v32_p, u_p = _proj(xp2, n1, win_b, qwc, kw, gmat)
    qt_s, _, k32_s, _, v32_s, u_s = _proj(xs2, n1, win_b, qwc, kw, gmat)

    cw = p["conv_w"]
    cb = p["conv_b"].reshape(1, CONV_DIM)
    lnw = p["conv_ln_w"].reshape(1, CONV_DIM)
    lnb = p["conv_ln_b"].reshape(1, CONV_DIM)
    oc_p = _pconv(u_p, batch, seq, cw, cb, lnw, lnb)
    oc_s, conv_s = _sconv(state_conv, u_s.reshape(db, ds, CONV_DIM), cw, cb, lnw, lnb)
    conv_p = u_p.reshape(batch, seq, CONV_DIM)[:, seq - (CONV_W - 1):]

    tq, tk = PATTN_TQ, KV_TILE
    bias_tiles = jnp.stack(
        [_toeplitz(_bias_table(rel_bias, (1 - c) * tk - (tk - 1), (1 - c) * tk + tq), tk, tq)
         for c in range(1 + tq // tk)], axis=1)
    swc = p["subln_w"].reshape(V_DIM, 1)
    a_p = _pattn(lam1, qt_p, kb_p, vt_p, bias_tiles, swc, batch, seq, lam_init)

    nrow = 2 * N_HEADS * ds
    eye = jnp.asarray(np.eye(N_HEADS, dtype=bool))

    def per_head_rows(t3):
        full = jnp.where(eye[:, None, None, None, :], t3[:, None, :, :, None], MASKED)
        full = jnp.broadcast_to(full, (N_HEADS, 2, ds, t3.shape[2], N_HEADS))
        return full.reshape(nrow, t3.shape[2] * N_HEADS)

    hm = per_head_rows(jnp.zeros((N_HEADS, ds, PAGE_SIZE), F32))
    blast = per_head_rows(_toeplitz(_bias_table(rel_bias, 1, PAGE_SIZE + ds)[:, ::-1], ds, PAGE_SIZE))
    bnew = per_head_rows(_toeplitz(_bias_table(rel_bias, -(ds - 1), ds)[:, ::-1], ds, ds))
    swr = p["subln_w"].reshape(1, V_DIM)
    q_s = (qt_s.T.astype(F32)).reshape(db, ds, w)
    kn3 = k32_s.reshape(db, ds * N_HEADS, V_DIM)
    vn3 = v32_s.reshape(db, ds * N_HEADS, V_DIM)
    a_s = _sattn(page_table, lam1, q_s, kn3, vn3, hm, blast, bnew, swr, cache_k, cache_v, lam_init)

    wo_b = p["w_out"].astype(BF16)
    n2 = p["norm2_w"].reshape(1, d)
    wr = jnp.concatenate([p["w_group"], p["w_expert"]], axis=1).T
    br = jnp.concatenate([p["b_group"], p["b_expert"]]).reshape(-1, 1)
    tri = jnp.asarray(np.triu(np.ones((MIX_TM, MIX_TM), np.float32), 1), BF16)
    cnt0 = jnp.zeros((N_EXPERTS, 1), F32)
    x2_p, h_p, route_p, cnt_p = _mix(a_p, oc_p, xp2, wo_b, n2, wr, br, tri, cnt0)
    x2_s, h_s, route_s, cnt = _mix(a_s.reshape(db * ds, w), oc_s.reshape(db * ds, CONV_DIM), xs2,
                                   wo_b, n2, wr, br, tri, cnt_p)

    bm = EXPERT_BM
    counts = cnt[:, 0].astype(jnp.int32)
    padded = (counts + bm - 1) // bm * bm
    pad_end = jnp.cumsum(padded)
    pad_start = pad_end - padded
    n_assign = 2 * (xp2.shape[0] + xs2.shape[0])
    n_blocks = -(-n_assign // bm) + N_EXPERTS
    nslot = n_blocks * bm
    block_first = jnp.arange(n_blocks, dtype=jnp.int32) * bm
    block_expert = jnp.minimum(
        jnp.sum((pad_end[None, :] <= block_first[:, None]).astype(jnp.int32), axis=1), N_EXPERTS - 1)
    n_used = (pad_end[-1] // bm).reshape(1).astype(jnp.int32)
    expert_ids = jnp.arange(N_EXPERTS, dtype=jnp.int32)[:, None, None]

    def slots_of(route):
        e = route[0:2].astype(jnp.int32)
        rank = route[2:4].astype(jnp.int32)
        s = jnp.sum(jnp.where(e[None] == expert_ids, pad_start[:, None, None], 0), axis=0) + rank
        t = s.shape[1]
        return s.reshape(2, t // ROW_TM, ROW_TM).transpose(1, 0, 2)

    slots_p, slots_s = slots_of(route_p), slots_of(route_s)

    xb = jnp.zeros((nslot, d), F32)
    xb = _dispatch(slots_p, h_p, xb)
    xb = _dispatch(slots_s, h_s, xb)
    yb = _experts(block_expert, n_used, xb, p["w1"].astype(BF16), p["w3"].astype(BF16), p["w2"].astype(BF16))
    y_p = _combine(slots_p, x2_p, route_p[4:6].T, yb)
    y_s = _combine(slots_s, x2_s, route_s[4:6].T, yb)

    kshape = (N_HEADS, V_DIM)
    return (y_p, y_s,
            k32_p.reshape(batch, seq, *kshape), v32_p.reshape(batch, seq, *kshape), conv_p,
            k32_s.reshape(db, ds, *kshape), v32_s.reshape(db, ds, *kshape), conv_s)


def kernel(x_prompt, x_sample, cache_k, cache_v, state_conv, page_table, rel_bias, norm1_w, w_in, q_norm_w, k_norm_w, lambda_q1, lambda_k1, lambda_q2, lambda_k2, subln_w, conv_w, conv_b, conv_ln_w, conv_ln_b, w_out, norm2_w, w_group, b_group, w_expert, b_expert, w1, w3, w2):
    batch, seq, d = x_prompt.shape
    db, ds, _ = x_sample.shape
    depth = w_in.shape[0]
    assert depth == 1, "paged caches are read once; a deeper trunk needs one cache slice per layer"
    xp = x_prompt.reshape(batch * seq, d)
    xs = x_sample.reshape(db * ds, d)
    l = 0
    lam_init = 0.8 - 0.6 * math.exp(-0.3 * l)
    lam = (jnp.exp(jnp.sum(lambda_q1[l].astype(F32) * lambda_k1[l].astype(F32)))
           - jnp.exp(jnp.sum(lambda_q2[l].astype(F32) * lambda_k2[l].astype(F32)))
           + lam_init)
    params = dict(norm1_w=norm1_w[l], w_in=w_in[l], q_norm_w=q_norm_w[l], k_norm_w=k_norm_w[l],
                  subln_w=subln_w[l], conv_w=conv_w[l], conv_b=conv_b[l], conv_ln_w=conv_ln_w[l],
                  conv_ln_b=conv_ln_b[l], w_out=w_out[l], norm2_w=norm2_w[l], w_group=w_group[l],
                  b_group=b_group[l], w_expert=w_expert[l], b_expert=b_expert[l],
                  w1=w1[l], w3=w3[l], w2=w2[l])
    y_p, y_s, k_p, v_p, c_p, k_s, v_s, c_s = _layer(
        xp, xs, batch, seq, db, ds, cache_k, cache_v, state_conv[l], page_table, rel_bias, lam, lam_init, params)
    return (y_p.reshape(batch, seq, d), y_s.reshape(db, ds, d),
            k_p[None], v_p[None], c_p[None], k_s[None], v_s[None], c_s[None])
```

```python
import functools
import math

import numpy as np
import jax
import jax.numpy as jnp
from jax import lax
from jax.experimental import pallas as pl
from jax.experimental.pallas import tpu as pltpu

F32 = jnp.float32
BF16 = jnp.bfloat16

N_HEADS = 4
HEAD_DIM = 64
V_DIM = 2 * HEAD_DIM
ATTN_WIDTH = N_HEADS * V_DIM
CONV_DIM = 512
CONV_W = 31
N_BUCKETS = 32
MAX_DISTANCE = 128
N_GROUPS = 8
EXPERTS_PER_GROUP = 8
N_EXPERTS = N_GROUPS * EXPERTS_PER_GROUP
PAGE_SIZE = 128
EPS = 1e-6
MASKED = -1e30

VMEM_LIMIT = 56 * 1024 * 1024


def _cparams(sem):
    return pltpu.CompilerParams(dimension_semantics=sem, vmem_limit_bytes=VMEM_LIMIT)


PROJ_TM = 512
KV_TILE = 512
LOG2E = math.log2(math.e)


def _proj_kernel(x_ref, n1_ref, win_ref, qwc_ref, kw_ref, g_ref,
                 qt_ref, kb_ref, k32_ref, vt_ref, v32_ref, u_ref):
    tm = x_ref.shape[0]
    x = x_ref[...]
    ms = jnp.mean(x * x, axis=-1, keepdims=True)
    xn = (x * lax.rsqrt(ms + EPS)) * n1_ref[...]
    z = jnp.dot(xn.astype(BF16), win_ref[...], preferred_element_type=F32)
    w = ATTN_WIDTH
    zq, zk, zv = z[:, :w], z[:, w:2 * w], z[:, 2 * w:3 * w]
    g1, g2 = z[:, 3 * w:3 * w + CONV_DIM], z[:, 3 * w + CONV_DIM:]

    kk = zk * zk
    hi = kk.astype(BF16)
    lo = (kk - hi.astype(F32)).astype(BF16)
    kms = (jnp.dot(hi, g_ref[...], preferred_element_type=F32)
           + jnp.dot(lo, g_ref[...], preferred_element_type=F32))
    kn = (zk * lax.rsqrt(kms + EPS)) * kw_ref[...]
    kb_ref[...] = kn.astype(BF16)
    for h in range(N_HEADS):
        k32_ref[pl.ds(h, tm, stride=N_HEADS), :] = kn[:, h * V_DIM:(h + 1) * V_DIM]
        v32_ref[pl.ds(h, tm, stride=N_HEADS), :] = zv[:, h * V_DIM:(h + 1) * V_DIM]
    zvt = zv.T
    for t in range(tm // KV_TILE):
        vt_ref[t] = zvt[:, t * KV_TILE:(t + 1) * KV_TILE].astype(BF16)

    q3 = zq.T.reshape(2 * N_HEADS, HEAD_DIM, tm)
    qms = jnp.mean(q3 * q3, axis=1, keepdims=True)
    qn = (q3 * lax.rsqrt(qms + EPS)).reshape(w, tm) * qwc_ref[...]
    qt_ref[...] = qn.astype(BF16)

    u_ref[...] = g1 * (1.0 / (1.0 + jnp.exp(-g2)))


def _proj(x2d, n1, win_b, qwc, kw, gmat):
    t = x2d.shape[0]
    tm = PROJ_TM
    assert t % tm == 0
    w = ATTN_WIDTH
    const = lambda shape: pl.BlockSpec(shape, lambda i: (0,) * len(shape))
    return pl.pallas_call(
        _proj_kernel,
        grid=(t // tm,),
        in_specs=[
            pl.BlockSpec((tm, x2d.shape[1]), lambda i: (i, 0)),
            const(n1.shape), const(win_b.shape), const(qwc.shape), const(kw.shape), const(gmat.shape),
        ],
        out_specs=[
            pl.BlockSpec((w, tm), lambda i: (0, i)),
            pl.BlockSpec((tm, w), lambda i: (i, 0)),
            pl.BlockSpec((tm * N_HEADS, V_DIM), lambda i: (i, 0)),
            pl.BlockSpec((tm // KV_TILE, w, KV_TILE), lambda i: (i, 0, 0)),
            pl.BlockSpec((tm * N_HEADS, V_DIM), lambda i: (i, 0)),
            pl.BlockSpec((tm, CONV_DIM), lambda i: (i, 0)),
        ],
        out_shape=[
            jax.ShapeDtypeStruct((w, t), BF16),
            jax.ShapeDtypeStruct((t, w), BF16),
            jax.ShapeDtypeStruct((t * N_HEADS, V_DIM), F32),
            jax.ShapeDtypeStruct((t // KV_TILE, w, KV_TILE), BF16),
            jax.ShapeDtypeStruct((t * N_HEADS, V_DIM), F32),
            jax.ShapeDtypeStruct((t, CONV_DIM), F32),
        ],
        compiler_params=_cparams(("arbitrary",)),
        name="proj",
    )(x2d, n1, win_b, qwc, kw, gmat)


CONV_TC = 512
CONV_RC = 64
CONV_HALO = 32


def _ln_swish(c, lnw, lnb):
    mu = jnp.mean(c, axis=-1, keepdims=True)
    d = c - mu
    var = jnp.mean(d * d, axis=-1, keepdims=True)
    cn = (d * lax.rsqrt(var + EPS)) * lnw + lnb
    return cn * (1.0 / (1.0 + jnp.exp(-cn)))


def _pconv_kernel(prev_ref, cur_ref, w_ref, b_ref, lnw_ref, lnb_ref, o_ref, win_ref):
    i = pl.program_id(1)
    tc = cur_ref.shape[0]
    n = CONV_HALO + tc
    win_ref[0, 0:CONV_HALO, :] = jnp.where(i > 0, prev_ref[...], 0.0)
    win_ref[0, CONV_HALO:n, :] = cur_ref[...]
    win_ref[0, n:n + 8, :] = jnp.zeros((8, CONV_DIM), F32)
    step = n // 4
    for r in range(1, 8):
        for c0 in range(0, n, step):
            win_ref[r, c0:c0 + step, :] = win_ref[0, c0 + r:c0 + r + step, :]
    off = CONV_HALO - (CONV_W - 1)

    def chunk(c, carry):
        base = pl.multiple_of(c * CONV_RC, CONV_RC)
        acc = jnp.zeros((CONV_RC, CONV_DIM), F32) + b_ref[...]
        for k in range(CONV_W):
            shift = k + off
            rows = pl.ds(pl.multiple_of(base + (shift // 8) * 8, 8), CONV_RC)
            acc = acc + w_ref[k:k + 1, :] * win_ref[shift % 8, rows, :]
        o_ref[pl.ds(base, CONV_RC), :] = _ln_swish(acc, lnw_ref[...], lnb_ref[...]).astype(o_ref.dtype)
        return carry

    lax.fori_loop(0, tc // CONV_RC, chunk, 0)


def _pconv(u2d, batch, seq, cw, cb, lnw, lnb):
    tc = CONV_TC
    assert seq % tc == 0
    nt = seq // tc
    const = lambda shape: pl.BlockSpec(shape, lambda b, i: (0,) * len(shape))
    return pl.pallas_call(
        _pconv_kernel,
        grid=(batch, nt),
        in_specs=[
            pl.BlockSpec((CONV_HALO, CONV_DIM),
                         lambda b, i: (jnp.maximum((b * seq + i * tc) // CONV_HALO - 1, 0), 0)),
            pl.BlockSpec((tc, CONV_DIM), lambda b, i: (b * nt + i, 0)),
            const(cw.shape), const(cb.shape), const(lnw.shape), const(lnb.shape),
        ],
        out_specs=pl.BlockSpec((tc, CONV_DIM), lambda b, i: (b * nt + i, 0)),
        out_shape=jax.ShapeDtypeStruct((batch * seq, CONV_DIM), BF16),
        scratch_shapes=[pltpu.VMEM((8, CONV_HALO + tc + 8, CONV_DIM), F32)],
        compiler_params=_cparams(("arbitrary", "arbitrary")),
        name="pconv",
    )(u2d, u2d, cw, cb, lnw, lnb)


SCONV_R = 8


def _sconv_kernel(st_ref, u_ref, w_ref, b_ref, lnw_ref, lnb_ref, o_ref, ns_ref, win_ref):
    nprev = st_ref.shape[1]
    ds = u_ref.shape[1]
    for r in range(st_ref.shape[0]):
        win_ref[0:nprev, :] = st_ref[r]
        win_ref[nprev:nprev + ds, :] = u_ref[r]
        acc = jnp.zeros((ds, CONV_DIM), F32) + b_ref[...]
        for k in range(CONV_W):
            acc = acc + w_ref[k:k + 1, :] * win_ref[k:k + ds, :]
        o_ref[r] = _ln_swish(acc, lnw_ref[...], lnb_ref[...])
        ns_ref[r] = win_ref[ds:ds + nprev, :]


def _sconv(state, u3, cw, cb, lnw, lnb):
    db, nprev, _ = state.shape
    ds = u3.shape[1]
    r = SCONV_R
    assert db % r == 0 and nprev == CONV_W - 1
    const = lambda shape: pl.BlockSpec(shape, lambda i: (0,) * len(shape))
    return pl.pallas_call(
        _sconv_kernel,
        grid=(db // r,),
        in_specs=[
            pl.BlockSpec((r, nprev, CONV_DIM), lambda i: (i, 0, 0)),
            pl.BlockSpec((r, ds, CONV_DIM), lambda i: (i, 0, 0)),
            const(cw.shape), const(cb.shape), const(lnw.shape), const(lnb.shape),
        ],
        out_specs=[
            pl.BlockSpec((r, ds, CONV_DIM), lambda i: (i, 0, 0)),
            pl.BlockSpec((r, nprev, CONV_DIM), lambda i: (i, 0, 0)),
        ],
        out_shape=[
            jax.ShapeDtypeStruct((db, ds, CONV_DIM), F32),
            jax.ShapeDtypeStruct((db, nprev, CONV_DIM), F32),
        ],
        scratch_shapes=[pltpu.VMEM((nprev + ds + 2, CONV_DIM), F32)],
        compiler_params=_cparams(("arbitrary",)),
        name="sconv",
    )(state, u3, cw, cb, lnw, lnb)


PATTN_TQ = KV_TILE


def _pattn_kernel(lam_ref, qt_ref, kb_ref, vt_ref, bias_ref, sw_ref, o_ref, m_ref, l_ref, acc_ref, s_ref,
                  *, lam_init):
    i = pl.program_id(2)
    tk = KV_TILE
    m_ref[...] = jnp.full(m_ref.shape, -jnp.inf, F32)
    l_ref[...] = jnp.zeros(l_ref.shape, F32)
    acc_ref[...] = jnp.zeros(acc_ref.shape, F32)

    def scores(c, buf):
        k0 = pl.multiple_of(c * tk, tk)
        for mm in range(2):
            kt = kb_ref[pl.ds(k0, tk), mm * HEAD_DIM:(mm + 1) * HEAD_DIM]
            s_ref[buf, mm] = jnp.dot(kt, qt_ref[mm * HEAD_DIM:(mm + 1) * HEAD_DIM, :],
                                     preferred_element_type=F32)

    def consume(c, buf, bias):
        ps, alphas = [], []
        for mm in range(2):
            s = s_ref[buf, mm]
            if bias is not None:
                s = s + bias
            m_prev = m_ref[mm]
            m_new = jnp.maximum(m_prev, jnp.max(s, axis=0, keepdims=True))
            alpha = jnp.exp2(m_prev - m_new)
            p = jnp.exp2(s - m_new)
            l_ref[mm] = alpha * l_ref[mm] + jnp.sum(p, axis=0, keepdims=True)
            m_ref[mm] = m_new
            ps.append(p.astype(BF16))
            alphas.append(alpha)
        for mm in range(2):
            pv = jnp.dot(vt_ref[c], ps[mm], preferred_element_type=F32)
            acc_ref[mm] = alphas[mm] * acc_ref[mm] + pv

    scores(0, 0)

    def far_pair(t, carry):
        scores(2 * t + 1, 1)
        consume(2 * t, 0, None)
        scores(2 * t + 2, 0)
        consume(2 * t + 1, 1, None)
        return carry

    def far_quad(t, carry):
        far_pair(2 * t, carry)
        far_pair(2 * t + 1, carry)
        return carry

    n_far = jnp.maximum(i - 1, 0)
    n_quad = n_far // 4
    lax.fori_loop(0, n_quad, far_quad, 0)
    lax.fori_loop(2 * n_quad, n_far // 2, far_pair, 0)

    @pl.when(i == 0)
    def _():
        consume(0, 0, bias_ref[1])

    @pl.when(i % 2 == 1)
    def _():
        scores(i, 1)
        consume(i - 1, 0, bias_ref[0])
        consume(i, 1, bias_ref[1])

    @pl.when((i % 2 == 0) & (i > 0))
    def _():
        scores(i - 1, 1)
        consume(i - 2, 0, None)
        scores(i, 0)
        consume(i - 1, 1, bias_ref[0])
        consume(i, 0, bias_ref[1])

    lam = lam_ref[0]
    o = acc_ref[0] / l_ref[0] - lam * (acc_ref[1] / l_ref[1])
    ms = jnp.mean(o * o, axis=0, keepdims=True)
    on = ((o * lax.rsqrt(ms + EPS)) * sw_ref[...]) * (1.0 - lam_init)
    o_ref[...] = on.T.astype(o_ref.dtype)


def _pattn(lam, qt, kb, vt, bias_tiles, swc, batch, seq, lam_init):
    tq = PATTN_TQ
    tk = KV_TILE
    assert seq % tq == 0 and tq == tk
    nq = seq // tq
    nkv = seq // tk
    return pl.pallas_call(
        functools.partial(_pattn_kernel, lam_init=lam_init),
        grid=(batch, N_HEADS, nq),
        in_specs=[
            pl.BlockSpec(memory_space=pltpu.SMEM),
            pl.BlockSpec((V_DIM, tq), lambda b, h, i: (h, b * nq + i)),
            pl.BlockSpec((seq, V_DIM), lambda b, h, i: (b, h)),
            pl.BlockSpec((nkv, V_DIM, tk), lambda b, h, i: (b, h, 0)),
            pl.BlockSpec((None, 1 + tq // tk, tk, tq), lambda b, h, i: (h, 0, 0, 0)),
            pl.BlockSpec(swc.shape, lambda b, h, i: (0, 0)),
        ],
        out_specs=pl.BlockSpec((tq, V_DIM), lambda b, h, i: (b * nq + i, h)),
        out_shape=jax.ShapeDtypeStruct((batch * seq, ATTN_WIDTH), BF16),
        scratch_shapes=[
            pltpu.VMEM((2, 1, tq), F32),
            pltpu.VMEM((2, 1, tq), F32),
            pltpu.VMEM((2, V_DIM, tq), F32),
            pltpu.VMEM((2, 2, tk, tq), F32),
        ],
        compiler_params=_cparams(("arbitrary", "arbitrary", "arbitrary")),
        name="pattn",
    )(lam, qt, kb, vt, bias_tiles, swc)


SATTN_CP = 8
SATTN_NBUF = 4
ROWS_PER_PAGE = PAGE_SIZE * N_HEADS


def _sattn_kernel(pt_ref, lam_ref, q_ref, kn_ref, vn_ref, hm_ref, blast_ref, bnew_ref, sw_ref,
                  ck_ref, cv_ref, o_ref, kbuf, vbuf, ksem, vsem, *, n_req, n_chunks, lam_init):
    r = pl.program_id(0)
    cp = SATTN_CP
    nbuf = SATTN_NBUF
    rows = cp * ROWS_PER_PAGE

    def page_copies(req, chunk, slot):
        out = []
        for p in range(cp):
            pg = pt_ref[req, chunk * cp + p]
            out.append(pltpu.make_async_copy(ck_ref.at[0, pg], kbuf.at[slot, p], ksem.at[slot]))
            out.append(pltpu.make_async_copy(cv_ref.at[0, pg], vbuf.at[slot, p], vsem.at[slot]))
        return out

    def start_chunk(g):
        req = g // n_chunks
        chunk = g % n_chunks
        for c in page_copies(req, chunk, g % nbuf):
            c.start()

    @pl.when(r == 0)
    def _():
        for g in range(nbuf - 1):
            start_chunk(g)

    qs = q_ref[0]
    ds = qs.shape[0]
    lane = lax.broadcasted_iota(jnp.int32, (ds, V_DIM), 1)
    pieces = []
    for h in range(N_HEADS):
        qh = qs[:, h * V_DIM:(h + 1) * V_DIM]
        for mm in range(2):
            keep = (lane < HEAD_DIM) if mm == 0 else (lane >= HEAD_DIM)
            pieces.append(jnp.where(keep, qh, 0.0))
    q2 = jnp.concatenate(pieces, axis=0).astype(BF16)
    nrow = q2.shape[0]

    def step(s, state, v2):
        m_prev, l_prev, acc = state
        m_new = jnp.maximum(m_prev, jnp.max(s, axis=1, keepdims=True))
        alpha = jnp.exp2(m_prev - m_new)
        p = jnp.exp2(s - m_new)
        l_new = alpha * l_prev + jnp.sum(p, axis=1, keepdims=True)
        acc_new = alpha * acc + jnp.dot(p.astype(BF16), v2, preferred_element_type=F32)
        return m_new, l_new, acc_new

    nt = (((1,), (1,)), ((), ()))
    hm = hm_ref[...]
    hm_chunk = jnp.concatenate([hm] * cp, axis=1)
    last_mask = jnp.concatenate([hm] * (cp - 1) + [blast_ref[...]], axis=1)

    def chunk_body(c, state):
        g = r * n_chunks + c
        nxt = g + (nbuf - 1)

        @pl.when(nxt < n_req * n_chunks)
        def _():
            start_chunk(nxt)

        slot = g % nbuf
        for cpy in page_copies(r, c, slot):
            cpy.wait()
        k2 = kbuf[slot].reshape(rows, V_DIM).astype(BF16)
        v2 = vbuf[slot].reshape(rows, V_DIM).astype(BF16)
        s = lax.dot_general(q2, k2, nt, preferred_element_type=F32)
        s = s + jnp.where(c == n_chunks - 1, last_mask, hm_chunk)
        return step(s, state, v2)

    state = (jnp.full((nrow, 1), -jnp.inf, F32), jnp.zeros((nrow, 1), F32), jnp.zeros((nrow, V_DIM), F32))
    state = lax.fori_loop(0, n_chunks, chunk_body, state)

    kn = kn_ref[0].astype(BF16)
    vn = vn_ref[0].astype(BF16)
    s = lax.dot_general(q2, kn, nt, preferred_element_type=F32) + bnew_ref[...]
    _, l_fin, acc = step(s, state, vn)

    lam = lam_ref[0]
    o = acc / l_fin
    outs = []
    for h in range(N_HEADS):
        o1 = o[(2 * h) * ds:(2 * h + 1) * ds]
        o2 = o[(2 * h + 1) * ds:(2 * h + 2) * ds]
        oh = o1 - lam * o2
        ms = jnp.mean(oh * oh, axis=-1, keepdims=True)
        outs.append(((oh * lax.rsqrt(ms + EPS)) * sw_ref[...]) * (1.0 - lam_init))
    o_ref[0] = jnp.concatenate(outs, axis=1)


def _sattn(page_table, lam, q3, kn3, vn3, hm, blast, bnew, swr, cache_k, cache_v, lam_init):
    n_req, ds, _ = q3.shape
    n_pages = page_table.shape[1]
    assert n_pages % SATTN_CP == 0
    n_chunks = n_pages // SATTN_CP
    assert n_chunks % SATTN_NBUF == 0 or SATTN_NBUF % n_chunks == 0 or True
    nrow = 2 * N_HEADS * ds
    rows = SATTN_CP * ROWS_PER_PAGE
    kernel = functools.partial(_sattn_kernel, n_req=n_req, n_chunks=n_chunks, lam_init=lam_init)
    const = lambda shape: pl.BlockSpec(shape, lambda i, pt: (0,) * len(shape))
    return pl.pallas_call(
        kernel,
        grid_spec=pltpu.PrefetchScalarGridSpec(
            num_scalar_prefetch=1,
            grid=(n_req,),
            in_specs=[
                pl.BlockSpec(memory_space=pltpu.SMEM),
                pl.BlockSpec((1, ds, ATTN_WIDTH), lambda i, pt: (i, 0, 0)),
                pl.BlockSpec((1, ds * N_HEADS, V_DIM), lambda i, pt: (i, 0, 0)),
                pl.BlockSpec((1, ds * N_HEADS, V_DIM), lambda i, pt: (i, 0, 0)),
                const(hm.shape), const(blast.shape), const(bnew.shape), const(swr.shape),
                pl.BlockSpec(memory_space=pl.ANY),
                pl.BlockSpec(memory_space=pl.ANY),
            ],
            out_specs=pl.BlockSpec((1, ds, ATTN_WIDTH), lambda i, pt: (i, 0, 0)),
            scratch_shapes=[
                pltpu.VMEM((SATTN_NBUF, SATTN_CP, PAGE_SIZE, N_HEADS, V_DIM), F32),
                pltpu.VMEM((SATTN_NBUF, SATTN_CP, PAGE_SIZE, N_HEADS, V_DIM), F32),
                pltpu.SemaphoreType.DMA((SATTN_NBUF,)),
                pltpu.SemaphoreType.DMA((SATTN_NBUF,)),
            ],
        ),
        out_shape=jax.ShapeDtypeStruct((n_req, ds, ATTN_WIDTH), F32),
        compiler_params=_cparams(("arbitrary",)),
        name="sattn",
    )(page_table, lam, q3, kn3, vn3, hm, blast, bnew, swr, cache_k, cache_v)


MIX_TM = 512
ROUTE_ROWS = 8
LANES = 128
SUBLANES = 8


def _store_row_tiles(ref, x):
    n, d = x.shape
    c = d // LANES
    for j in range(c):
        ref[pl.ds(j, n, stride=c), :] = x[:, j * LANES:(j + 1) * LANES]


def _load_row_tiles(ref, n, c):
    return jnp.concatenate([ref[pl.ds(j, n, stride=c), :] for j in range(c)], axis=1)


def _mix_kernel(a_ref, c_ref, x_ref, wo_ref, n2_ref, wr_ref, br_ref, tri_ref, cin_ref,
                x2_ref, h_ref, route_ref, cnt_ref):
    i = pl.program_id(0)
    tm = x_ref.shape[0]
    half = a_ref.shape[1]

    @pl.when(i == 0)
    def _():
        cnt_ref[...] = cin_ref[...]

    x2 = (x_ref[...]
          + jnp.dot(a_ref[...].astype(BF16), wo_ref[0:half, :], preferred_element_type=F32)
          + jnp.dot(c_ref[...].astype(BF16), wo_ref[half:, :], preferred_element_type=F32))
    x2_ref[...] = x2
    ms = jnp.mean(x2 * x2, axis=-1, keepdims=True)
    h = (x2 * lax.rsqrt(ms + EPS)) * n2_ref[...]
    _store_row_tiles(h_ref, h)

    h_hi = h.astype(BF16)
    h_lo = (h - h_hi.astype(F32)).astype(BF16)
    wr = wr_ref[...]
    w_hi = wr.astype(BF16)
    w_lo = (wr - w_hi.astype(F32)).astype(BF16)
    nt = (((1,), (1,)), ((), ()))
    logits = (lax.dot_general(w_hi, h_hi, nt, preferred_element_type=F32)
              + lax.dot_general(w_hi, h_lo, nt, preferred_element_type=F32)
              + lax.dot_general(w_lo, h_hi, nt, preferred_element_type=F32)) + br_ref[...]
    gl = logits[0:N_GROUPS]
    el = logits[N_GROUPS:N_GROUPS + N_EXPERTS]

    sub8 = lax.broadcasted_iota(jnp.int32, (N_GROUPS, tm), 0)

    def first_argmax(vals):
        top = jnp.max(vals, axis=0, keepdims=True)
        idx = jnp.min(jnp.where(vals == top, sub8, N_GROUPS), axis=0, keepdims=True)
        return top, idx

    gexp = jnp.exp(gl - jnp.max(gl, axis=0, keepdims=True))
    gp = gexp / jnp.sum(gexp, axis=0, keepdims=True)
    g_prob, g_idx = first_argmax(gp)

    el3 = el.reshape(N_GROUPS, EXPERTS_PER_GROUP, tm)
    sel = jnp.zeros((EXPERTS_PER_GROUP, tm), F32)
    for g in range(N_GROUPS):
        sel = sel + jnp.where(g_idx == g, el3[g], 0.0)
    v1, i1 = first_argmax(sel)
    v2, i2 = first_argmax(jnp.where(sub8 == i1, -jnp.inf, sel))
    e2 = jnp.exp(v2 - v1)
    p1 = 1.0 / (1.0 + e2)
    p2 = e2 / (1.0 + e2)
    ea = g_idx * EXPERTS_PER_GROUP + i1
    eb = g_idx * EXPERTS_PER_GROUP + i2

    sub64 = lax.broadcasted_iota(jnp.int32, (N_EXPERTS, tm), 0)
    oh_a = sub64 == ea
    oh_b = sub64 == eb
    oh = jnp.where(oh_a | oh_b, 1.0, 0.0)
    before = jnp.dot(oh.astype(BF16), tri_ref[...], preferred_element_type=F32) + cnt_ref[...]
    rank_a = jnp.sum(jnp.where(oh_a, before, 0.0), axis=0, keepdims=True)
    rank_b = jnp.sum(jnp.where(oh_b, before, 0.0), axis=0, keepdims=True)
    cnt_ref[...] = cnt_ref[...] + jnp.sum(oh, axis=1, keepdims=True)

    route_ref[...] = jnp.concatenate(
        [ea.astype(F32), eb.astype(F32), rank_a, rank_b, g_prob * p1, g_prob * p2,
         jnp.zeros((ROUTE_ROWS - 6, tm), F32)], axis=0)


def _mix(a, c, x2d, wo_b, n2, wr, br, tri, cnt_in):
    t, d = x2d.shape
    tm = MIX_TM
    assert t % tm == 0
    const = lambda shape: pl.BlockSpec(shape, lambda i: (0,) * len(shape))
    return pl.pallas_call(
        _mix_kernel,
        grid=(t // tm,),
        in_specs=[
            pl.BlockSpec((tm, a.shape[1]), lambda i: (i, 0)),
            pl.BlockSpec((tm, c.shape[1]), lambda i: (i, 0)),
            pl.BlockSpec((tm, d), lambda i: (i, 0)),
            const(wo_b.shape), const(n2.shape), const(wr.shape), const(br.shape), const(tri.shape),
            const(cnt_in.shape),
        ],
        out_specs=[
            pl.BlockSpec((tm, d), lambda i: (i, 0)),
            pl.BlockSpec((tm * (d // LANES), LANES), lambda i: (i, 0)),
            pl.BlockSpec((ROUTE_ROWS, tm), lambda i: (0, i)),
            const(cnt_in.shape),
        ],
        out_shape=[
            jax.ShapeDtypeStruct((t, d), F32),
            jax.ShapeDtypeStruct((t * (d // LANES), LANES), F32),
            jax.ShapeDtypeStruct((ROUTE_ROWS, t), F32),
            jax.ShapeDtypeStruct(cnt_in.shape, F32),
        ],
        compiler_params=_cparams(("arbitrary",)),
        name="mix",
    )(a, c, x2d, wo_b, n2, wr, br, tri, cnt_in)


ROW_TM = 256
ROW_UNROLL = 8
EXPERT_BM = 256


def _row_tile(row):
    return pl.ds(pl.multiple_of(row * SUBLANES, SUBLANES), SUBLANES)


def _dispatch_kernel(slots_ref, h_ref, xin_ref, xb_ref, sem):
    del xin_ref
    tm = slots_ref.shape[2]

    def issue(t, carry):
        for j in range(2):
            s = slots_ref[0, j, t]
            pltpu.make_async_copy(h_ref.at[_row_tile(t)], xb_ref.at[_row_tile(s)], sem).start(priority=j)
        return carry

    lax.fori_loop(0, tm, issue, 0, unroll=ROW_UNROLL)

    def drain(t, carry):
        for j in range(2):
            pltpu.make_async_copy(h_ref.at[_row_tile(0)], xb_ref.at[_row_tile(0)], sem).wait()
        return carry

    lax.fori_loop(0, tm, drain, 0, unroll=ROW_UNROLL)


def _dispatch(slots3, h, xb):
    tm = ROW_TM
    assert h.shape[0] % (tm * SUBLANES) == 0 and h.shape[1] == LANES
    return pl.pallas_call(
        _dispatch_kernel,
        grid=(h.shape[0] // (tm * SUBLANES),),
        in_specs=[
            pl.BlockSpec((1, 2, tm), lambda i: (i, 0, 0), memory_space=pltpu.SMEM),
            pl.BlockSpec((tm * SUBLANES, LANES), lambda i: (i, 0)),
            pl.BlockSpec(memory_space=pl.ANY),
        ],
        out_specs=pl.BlockSpec(memory_space=pl.ANY),
        out_shape=jax.ShapeDtypeStruct(xb.shape, xb.dtype),
        scratch_shapes=[pltpu.SemaphoreType.DMA],
        input_output_aliases={2: 0},
        compiler_params=_cparams(("arbitrary",)),
        name="dispatch",
    )(slots3, h, xb)


def _experts_kernel(be_ref, nb_ref, xb_ref, w1_ref, w3_ref, w2_ref, yb_ref, w1b, w3b, w2b):
    i = pl.program_id(0)
    bm = EXPERT_BM

    @pl.when((i == 0) | (be_ref[i] != be_ref[jnp.maximum(i - 1, 0)]))
    def _():
        w1b[...] = w1_ref[0].astype(BF16)
        w3b[...] = w3_ref[0].astype(BF16)
        w2b[...] = w2_ref[0].astype(BF16)

    @pl.when(i < nb_ref[0])
    def _():
        x = _load_row_tiles(xb_ref, bm, SUBLANES).astype(BF16)
        a = jnp.dot(x, w1b[...], preferred_element_type=F32)
        b = jnp.dot(x, w3b[...], preferred_element_type=F32)
        mid = (a * (1.0 / (1.0 + jnp.exp(-a)))) * b
        _store_row_tiles(yb_ref, jnp.dot(mid.astype(BF16), w2b[...], preferred_element_type=F32))

    @pl.when(i >= nb_ref[0])
    def _():
        yb_ref[...] = jnp.zeros(yb_ref.shape, F32)


def _experts(block_expert, n_used, xb, w1, w3, w2):
    bm = EXPERT_BM
    _, d, hid = w1.shape
    rows = bm * SUBLANES
    return pl.pallas_call(
        _experts_kernel,
        grid_spec=pltpu.PrefetchScalarGridSpec(
            num_scalar_prefetch=2,
            grid=(xb.shape[0] // rows,),
            in_specs=[
                pl.BlockSpec((rows, LANES), lambda i, be, nb: (i, 0)),
                pl.BlockSpec((1, d, hid), lambda i, be, nb: (be[i], 0, 0)),
                pl.BlockSpec((1, d, hid), lambda i, be, nb: (be[i], 0, 0)),
                pl.BlockSpec((1, hid, d), lambda i, be, nb: (be[i], 0, 0)),
            ],
            out_specs=pl.BlockSpec((rows, LANES), lambda i, be, nb: (i, 0)),
            scratch_shapes=[
                pltpu.VMEM((d, hid), BF16),
                pltpu.VMEM((d, hid), BF16),
                pltpu.VMEM((hid, d), BF16),
            ],
        ),
        out_shape=jax.ShapeDtypeStruct(xb.shape, F32),
        compiler_params=_cparams(("arbitrary",)),
        name="experts",
    )(block_expert, n_used, xb, w1, w3, w2)


def _combine_kernel(slots_ref, x2_ref, g_ref, yb_ref, o_ref, ybuf, sem):
    tm = x2_ref.shape[0]

    def issue(t, carry):
        for j in range(2):
            s = slots_ref[0, j, t]
            pltpu.make_async_copy(yb_ref.at[_row_tile(s)], ybuf.at[j, _row_tile(t)], sem).start(priority=j)
        return carry

    lax.fori_loop(0, tm, issue, 0, unroll=ROW_UNROLL)

    def drain(t, carry):
        for j in range(2):
            pltpu.make_async_copy(yb_ref.at[_row_tile(0)], ybuf.at[j, _row_tile(0)], sem).wait()
        return carry

    lax.fori_loop(0, tm, drain, 0, unroll=ROW_UNROLL)
    g = g_ref[...]
    ya = _load_row_tiles(ybuf.at[0], tm, SUBLANES)
    yb = _load_row_tiles(ybuf.at[1], tm, SUBLANES)
    o_ref[...] = x2_ref[...] + (g[:, 0:1] * ya + g[:, 1:2] * yb)


def _combine(slots3, x2, gates, yb):
    t, d = x2.shape
    tm = ROW_TM
    assert t % tm == 0
    return pl.pallas_call(
        _combine_kernel,
        grid=(t // tm,),
        in_specs=[
            pl.BlockSpec((1, 2, tm), lambda i: (i, 0, 0), memory_space=pltpu.SMEM),
            pl.BlockSpec((tm, d), lambda i: (i, 0)),
            pl.BlockSpec((tm, 2), lambda i: (i, 0)),
            pl.BlockSpec(memory_space=pl.ANY),
        ],
        out_specs=pl.BlockSpec((tm, d), lambda i: (i, 0)),
        out_shape=jax.ShapeDtypeStruct((t, d), F32),
        scratch_shapes=[pltpu.VMEM((2, tm * SUBLANES, LANES), F32), pltpu.SemaphoreType.DMA],
        compiler_params=_cparams(("arbitrary",)),
        name="combine",
    )(slots3, x2, gates, yb)


def _t5_bucket(dist):
    n = np.maximum(dist, 0)
    max_exact = N_BUCKETS // 2
    nf = np.maximum(n, 1).astype(np.float32)
    large = max_exact + (np.log(nf / np.float32(max_exact)) / np.float32(math.log(MAX_DISTANCE / max_exact))
                         * np.float32(N_BUCKETS - max_exact)).astype(np.int32)
    large = np.minimum(large, N_BUCKETS - 1)
    return np.where(n < max_exact, n, large)


def _bias_table(rel_bias, d_lo, d_hi):
    dist = np.arange(d_lo, d_hi)
    onehot = (_t5_bucket(dist)[None, :] == np.arange(N_BUCKETS)[:, None]).astype(np.float32)
    tab = jnp.sum(rel_bias.T[:, :, None] * onehot[None], axis=1)
    tab = (tab - rel_bias[N_BUCKETS - 1][:, None]) * LOG2E
    return jnp.where(jnp.asarray(dist >= 0)[None, :], tab, MASKED)


def _toeplitz(v, nr, nc):
    n = nr + nc - 1
    assert v.shape[1] == n
    w = jnp.concatenate([v, jnp.zeros((v.shape[0], 1), v.dtype)], axis=1)
    flat = jnp.tile(w, (1, nr))[:, :nr * n]
    return flat.reshape(v.shape[0], nr, n)[:, :, nr - 1:nr - 1 + nc]


def _layer(xp2, xs2, batch, seq, db, ds, cache_k, cache_v, state_conv, page_table, rel_bias, lam, lam_init, p):
    d = xp2.shape[1]
    w = ATTN_WIDTH
    qwc = (jnp.tile(p["q_norm_w"], 2 * N_HEADS) * (HEAD_DIM ** -0.5 * LOG2E)).reshape(w, 1)
    kw = jnp.tile(p["k_norm_w"], 2 * N_HEADS).reshape(1, w)
    gid = np.arange(w) // HEAD_DIM
    gmat = jnp.asarray((gid[:, None] == gid[None, :]).astype(np.float32) / HEAD_DIM, BF16)
    n1 = p["norm1_w"].reshape(1, d)
    win_b = p["w_in"].astype(BF16)
    lam1 = lam.reshape(1).astype(F32)

    qt_p, kb_p, k32_p, vt_p, v32_p, u_p = _proj(xp2, n1, win_b, qwc, kw, gmat)
    qt_s, _, k32_s, _, v32_s, u_s = _proj(xs2, n1, win_b, qwc, kw, gmat)

    cw = p["conv_w"]
    cb = p["conv_b"].reshape(1, CONV_DIM)
    lnw = p["conv_ln_w"].reshape(1, CONV_DIM)
    lnb = p["conv_ln_b"].reshape(1, CONV_DIM)
    oc_p = _pconv(u_p, batch, seq, cw, cb, lnw, lnb)
    oc_s, conv_s = _sconv(state_conv, u_s.reshape(db, ds, CONV_DIM), cw, cb, lnw, lnb)
    conv_p = u_p.reshape(batch, seq, CONV_DIM)[:, seq - (CONV_W - 1):]

    tq, tk = PATTN_TQ, KV_TILE
    bias_tiles = jnp.stack(
        [_toeplitz(_bias_table(rel_bias, (1 - c) * tk - (tk - 1), (1 - c) * tk + tq), tk, tq)
         for c in range(1 + tq // tk)], axis=1)
    swc = p["subln_w"].reshape(V_DIM, 1)
    a_p = _pattn(lam1, qt_p, kb_p, vt_p, bias_tiles, swc, batch, seq, lam_init)

    nrow = 2 * N_HEADS * ds
    eye = jnp.asarray(np.eye(N_HEADS, dtype=bool))

    def per_head_rows(t3):
        full = jnp.where(eye[:, None, None, None, :], t3[:, None, :, :, None], MASKED)
        full = jnp.broadcast_to(full, (N_HEADS, 2, ds, t3.shape[2], N_HEADS))
        return full.reshape(nrow, t3.shape[2] * N_HEADS)

    hm = per_head_rows(jnp.zeros((N_HEADS, ds, PAGE_SIZE), F32))
    blast = per_head_rows(_toeplitz(_bias_table(rel_bias, 1, PAGE_SIZE + ds)[:, ::-1], ds, PAGE_SIZE))
    bnew = per_head_rows(_toeplitz(_bias_table(rel_bias, -(ds - 1), ds)[:, ::-1], ds, ds))
    swr = p["subln_w"].reshape(1, V_DIM)
    q_s = (qt_s.T.astype(F32)).reshape(db, ds, w)
    kn3 = k32_s.reshape(db, ds * N_HEADS, V_DIM)
    vn3 = v32_s.reshape(db, ds * N_HEADS, V_DIM)
    a_s = _sattn(page_table, lam1, q_s, kn3, vn3, hm, blast, bnew, swr, cache_k, cache_v, lam_init)

    wo_b = p["w_out"].astype(BF16)
    n2 = p["norm2_w"].reshape(1, d)
    wr = jnp.concatenate([p["w_group"], p["w_expert"]], axis=1).T
    br = jnp.concatenate([p["b_group"], p["b_expert"]]).reshape(-1, 1)
    tri = jnp.asarray(np.triu(np.ones((MIX_TM, MIX_TM), np.float32), 1), BF16)
    cnt0 = jnp.zeros((N_EXPERTS, 1), F32)
    x2_p, h_p, route_p, cnt_p = _mix(a_p, oc_p, xp2, wo_b, n2, wr, br, tri, cnt0)
    x2_s, h_s, route_s, cnt = _mix(a_s.reshape(db * ds, w), oc_s.reshape(db * ds, CONV_DIM), xs2,
                                   wo_b, n2, wr, br, tri, cnt_p)

    bm = EXPERT_BM
    counts = cnt[:, 0].astype(jnp.int32)
    padded = (counts + bm - 1) // bm * bm
    pad_end = jnp.cumsum(padded)
    pad_start = pad_end - padded
    n_assign = 2 * (xp2.shape[0] + xs2.shape[0])
    n_blocks = -(-n_assign // bm) + N_EXPERTS
    nslot = n_blocks * bm
    block_first = jnp.arange(n_blocks, dtype=jnp.int32) * bm
    block_expert = jnp.minimum(
        jnp.sum((pad_end[None, :] <= block_first[:, None]).astype(jnp.int32), axis=1), N_EXPERTS - 1)
    n_used = (pad_end[-1] // bm).reshape(1).astype(jnp.int32)
    expert_ids = jnp.arange(N_EXPERTS, dtype=jnp.int32)[:, None, None]

    def slots_of(route):
        e = route[0:2].astype(jnp.int32)
        rank = route[2:4].astype(jnp.int32)
        s = jnp.sum(jnp.where(e[None] == expert_ids, pad_start[:, None, None], 0), axis=0) + rank
        t = s.shape[1]
        return s.reshape(2, t // ROW_TM, ROW_TM).transpose(1, 0, 2)

    slots_p, slots_s = slots_of(route_p), slots_of(route_s)

    assert d == SUBLANES * LANES, "a model row must be exactly one 8x128 tile for the row-tile DMAs"
    xb = jnp.zeros((nslot * SUBLANES, LANES), F32)
    xb = _dispatch(slots_p, h_p, xb)
    xb = _dispatch(slots_s, h_s, xb)
    yb = _experts(block_expert, n_used, xb, p["w1"], p["w3"], p["w2"])
    y_p = _combine(slots_p, x2_p, route_p[4:6].T, yb)
    y_s = _combine(slots_s, x2_s, route_s[4:6].T, yb)

    kshape = (N_HEADS, V_DIM)
    return (y_p, y_s,
            k32_p.reshape(batch, seq, *kshape), v32_p.reshape(batch, seq, *kshape), conv_p,
            k32_s.reshape(db, ds, *kshape), v32_s.reshape(db, ds, *kshape), conv_s)


def kernel(x_prompt, x_sample, cache_k, cache_v, state_conv, page_table, rel_bias, norm1_w, w_in, q_norm_w, k_norm_w, lambda_q1, lambda_k1, lambda_q2, lambda_k2, subln_w, conv_w, conv_b, conv_ln_w, conv_ln_b, w_out, norm2_w, w_group, b_group, w_expert, b_expert, w1, w3, w2):
    batch, seq, d = x_prompt.shape
    db, ds, _ = x_sample.shape
    depth = w_in.shape[0]
    assert depth == 1, "paged caches are read once; a deeper trunk needs one cache slice per layer"
    xp = x_prompt.reshape(batch * seq, d)
    xs = x_sample.reshape(db * ds, d)
    l = 0
    lam_init = 0.8 - 0.6 * math.exp(-0.3 * l)
    lam = (jnp.exp(jnp.sum(lambda_q1[l].astype(F32) * lambda_k1[l].astype(F32)))
           - jnp.exp(jnp.sum(lambda_q2[l].astype(F32) * lambda_k2[l].astype(F32)))
           + lam_init)
    params = dict(norm1_w=norm1_w[l], w_in=w_in[l], q_norm_w=q_norm_w[l], k_norm_w=k_norm_w[l],
                  subln_w=subln_w[l], conv_w=conv_w[l], conv_b=conv_b[l], conv_ln_w=conv_ln_w[l],
                  conv_ln_b=conv_ln_b[l], w_out=w_out[l], norm2_w=norm2_w[l], w_group=w_group[l],
                  b_group=b_group[l], w_expert=w_expert[l], b_expert=b_expert[l],
                  w1=w1[l], w3=w3[l], w2=w2[l])
    y_p, y_s, k_p, v_p, c_p, k_s, v_s, c_s = _layer(
        xp, xs, batch, seq, db, ds, cache_k, cache_v, state_conv[l], page_table, rel_bias, lam, lam_init, params)
    return (y_p.reshape(batch, seq, d), y_s.reshape(db, ds, d),
            k_p[None], v_p[None], c_p[None], k_s[None], v_s[None], c_s[None])
```

```python
import functools
import math

import numpy as np
import jax
import jax.numpy as jnp
from jax import lax
from jax.experimental import pallas as pl
from jax.experimental.pallas import tpu as pltpu

F32 = jnp.float32
BF16 = jnp.bfloat16

N_HEADS = 4
HEAD_DIM = 64
V_DIM = 2 * HEAD_DIM
ATTN_WIDTH = N_HEADS * V_DIM
CONV_DIM = 512
CONV_W = 31
N_BUCKETS = 32
MAX_DISTANCE = 128
N_GROUPS = 8
EXPERTS_PER_GROUP = 8
N_EXPERTS = N_GROUPS * EXPERTS_PER_GROUP
PAGE_SIZE = 128
EPS = 1e-6
MASKED = -1e30

VMEM_LIMIT = 56 * 1024 * 1024


def _cparams(sem):
    return pltpu.CompilerParams(dimension_semantics=sem, vmem_limit_bytes=VMEM_LIMIT)


PROJ_TM = 512
KV_TILE = 512
LOG2E = math.log2(math.e)


def _proj_kernel(x_ref, n1_ref, win_ref, qwc_ref, kw_ref, g_ref,
                 qt_ref, kb_ref, k32_ref, vt_ref, v32_ref, u_ref):
    tm = x_ref.shape[0]
    x = x_ref[...]
    ms = jnp.mean(x * x, axis=-1, keepdims=True)
    xn = (x * lax.rsqrt(ms + EPS)) * n1_ref[...]
    z = jnp.dot(xn.astype(BF16), win_ref[...], preferred_element_type=F32)
    w = ATTN_WIDTH
    zq, zk, zv = z[:, :w], z[:, w:2 * w], z[:, 2 * w:3 * w]
    g1, g2 = z[:, 3 * w:3 * w + CONV_DIM], z[:, 3 * w + CONV_DIM:]

    kk = zk * zk
    hi = kk.astype(BF16)
    lo = (kk - hi.astype(F32)).astype(BF16)
    kms = (jnp.dot(hi, g_ref[...], preferred_element_type=F32)
           + jnp.dot(lo, g_ref[...], preferred_element_type=F32))
    kn = (zk * lax.rsqrt(kms + EPS)) * kw_ref[...]
    kb_ref[...] = kn.astype(BF16)
    for h in range(N_HEADS):
        k32_ref[pl.ds(h, tm, stride=N_HEADS), :] = kn[:, h * V_DIM:(h + 1) * V_DIM]
        v32_ref[pl.ds(h, tm, stride=N_HEADS), :] = zv[:, h * V_DIM:(h + 1) * V_DIM]
    zvt = zv.T
    for t in range(tm // KV_TILE):
        vt_ref[t] = zvt[:, t * KV_TILE:(t + 1) * KV_TILE].astype(BF16)

    q3 = zq.T.reshape(2 * N_HEADS, HEAD_DIM, tm)
    qms = jnp.mean(q3 * q3, axis=1, keepdims=True)
    qn = (q3 * lax.rsqrt(qms + EPS)).reshape(w, tm) * qwc_ref[...]
    qt_ref[...] = qn.astype(BF16)

    u_ref[...] = g1 * (1.0 / (1.0 + jnp.exp(-g2)))


def _proj(x2d, n1, win_b, qwc, kw, gmat):
    t = x2d.shape[0]
    tm = PROJ_TM
    assert t % tm == 0
    w = ATTN_WIDTH
    const = lambda shape: pl.BlockSpec(shape, lambda i: (0,) * len(shape))
    return pl.pallas_call(
        _proj_kernel,
        grid=(t // tm,),
        in_specs=[
            pl.BlockSpec((tm, x2d.shape[1]), lambda i: (i, 0)),
            const(n1.shape), const(win_b.shape), const(qwc.shape), const(kw.shape), const(gmat.shape),
        ],
        out_specs=[
            pl.BlockSpec((w, tm), lambda i: (0, i)),
            pl.BlockSpec((tm, w), lambda i: (i, 0)),
            pl.BlockSpec((tm * N_HEADS, V_DIM), lambda i: (i, 0)),
            pl.BlockSpec((tm // KV_TILE, w, KV_TILE), lambda i: (i, 0, 0)),
            pl.BlockSpec((tm * N_HEADS, V_DIM), lambda i: (i, 0)),
            pl.BlockSpec((tm, CONV_DIM), lambda i: (i, 0)),
        ],
        out_shape=[
            jax.ShapeDtypeStruct((w, t), BF16),
            jax.ShapeDtypeStruct((t, w), BF16),
            jax.ShapeDtypeStruct((t * N_HEADS, V_DIM), F32),
            jax.ShapeDtypeStruct((t // KV_TILE, w, KV_TILE), BF16),
            jax.ShapeDtypeStruct((t * N_HEADS, V_DIM), F32),
            jax.ShapeDtypeStruct((t, CONV_DIM), F32),
        ],
        compiler_params=_cparams(("arbitrary",)),
        name="proj",
    )(x2d, n1, win_b, qwc, kw, gmat)


CONV_TC = 512
CONV_RC = 64
CONV_HALO = 32


def _ln_swish(c, lnw, lnb):
    mu = jnp.mean(c, axis=-1, keepdims=True)
    d = c - mu
    var = jnp.mean(d * d, axis=-1, keepdims=True)
    cn = (d * lax.rsqrt(var + EPS)) * lnw + lnb
    return cn * (1.0 / (1.0 + jnp.exp(-cn)))


def _pconv_kernel(prev_ref, cur_ref, w_ref, b_ref, lnw_ref, lnb_ref, o_ref, win_ref):
    i = pl.program_id(1)
    tc = cur_ref.shape[0]
    n = CONV_HALO + tc
    win_ref[0, 0:CONV_HALO, :] = jnp.where(i > 0, prev_ref[...], 0.0)
    win_ref[0, CONV_HALO:n, :] = cur_ref[...]
    win_ref[0, n:n + 8, :] = jnp.zeros((8, CONV_DIM), F32)
    step = n // 4
    for r in range(1, 8):
        for c0 in range(0, n, step):
            win_ref[r, c0:c0 + step, :] = win_ref[0, c0 + r:c0 + r + step, :]
    off = CONV_HALO - (CONV_W - 1)

    def chunk(c, carry):
        base = pl.multiple_of(c * CONV_RC, CONV_RC)
        acc = jnp.zeros((CONV_RC, CONV_DIM), F32) + b_ref[...]
        for k in range(CONV_W):
            shift = k + off
            rows = pl.ds(pl.multiple_of(base + (shift // 8) * 8, 8), CONV_RC)
            acc = acc + w_ref[k:k + 1, :] * win_ref[shift % 8, rows, :]
        o_ref[pl.ds(base, CONV_RC), :] = _ln_swish(acc, lnw_ref[...], lnb_ref[...]).astype(o_ref.dtype)
        return carry

    lax.fori_loop(0, tc // CONV_RC, chunk, 0)


def _pconv(u2d, batch, seq, cw, cb, lnw, lnb):
    tc = CONV_TC
    assert seq % tc == 0
    nt = seq // tc
    const = lambda shape: pl.BlockSpec(shape, lambda b, i: (0,) * len(shape))
    return pl.pallas_call(
        _pconv_kernel,
        grid=(batch, nt),
        in_specs=[
            pl.BlockSpec((CONV_HALO, CONV_DIM),
                         lambda b, i: (jnp.maximum((b * seq + i * tc) // CONV_HALO - 1, 0), 0)),
            pl.BlockSpec((tc, CONV_DIM), lambda b, i: (b * nt + i, 0)),
            const(cw.shape), const(cb.shape), const(lnw.shape), const(lnb.shape),
        ],
        out_specs=pl.BlockSpec((tc, CONV_DIM), lambda b, i: (b * nt + i, 0)),
        out_shape=jax.ShapeDtypeStruct((batch * seq, CONV_DIM), BF16),
        scratch_shapes=[pltpu.VMEM((8, CONV_HALO + tc + 8, CONV_DIM), F32)],
        compiler_params=_cparams(("arbitrary", "arbitrary")),
        name="pconv",
    )(u2d, u2d, cw, cb, lnw, lnb)


SCONV_R = 8


def _sconv_kernel(st_ref, u_ref, w_ref, b_ref, lnw_ref, lnb_ref, o_ref, ns_ref, win_ref):
    nprev = st_ref.shape[1]
    ds = u_ref.shape[1]
    for r in range(st_ref.shape[0]):
        win_ref[0:nprev, :] = st_ref[r]
        win_ref[nprev:nprev + ds, :] = u_ref[r]
        acc = jnp.zeros((ds, CONV_DIM), F32) + b_ref[...]
        for k in range(CONV_W):
            acc = acc + w_ref[k:k + 1, :] * win_ref[k:k + ds, :]
        o_ref[r] = _ln_swish(acc, lnw_ref[...], lnb_ref[...])
        ns_ref[r] = win_ref[ds:ds + nprev, :]


def _sconv(state, u3, cw, cb, lnw, lnb):
    db, nprev, _ = state.shape
    ds = u3.shape[1]
    r = SCONV_R
    assert db % r == 0 and nprev == CONV_W - 1
    const = lambda shape: pl.BlockSpec(shape, lambda i: (0,) * len(shape))
    return pl.pallas_call(
        _sconv_kernel,
        grid=(db // r,),
        in_specs=[
            pl.BlockSpec((r, nprev, CONV_DIM), lambda i: (i, 0, 0)),
            pl.BlockSpec((r, ds, CONV_DIM), lambda i: (i, 0, 0)),
            const(cw.shape), const(cb.shape), const(lnw.shape), const(lnb.shape),
        ],
        out_specs=[
            pl.BlockSpec((r, ds, CONV_DIM), lambda i: (i, 0, 0)),
            pl.BlockSpec((r, nprev, CONV_DIM), lambda i: (i, 0, 0)),
        ],
        out_shape=[
            jax.ShapeDtypeStruct((db, ds, CONV_DIM), F32),
            jax.ShapeDtypeStruct((db, nprev, CONV_DIM), F32),
        ],
        scratch_shapes=[pltpu.VMEM((nprev + ds + 2, CONV_DIM), F32)],
        compiler_params=_cparams(("arbitrary",)),
        name="sconv",
    )(state, u3, cw, cb, lnw, lnb)


PATTN_TQ = KV_TILE


def _pattn_step(i, lam_ref, qt_ref, kb_ref, vt_ref, bias_ref, sw_ref, o_ref, m_ref, l_ref, acc_ref, s_ref,
                lam_init):
    tk = KV_TILE
    m_ref[...] = jnp.full(m_ref.shape, -jnp.inf, F32)
    l_ref[...] = jnp.zeros(l_ref.shape, F32)
    acc_ref[...] = jnp.zeros(acc_ref.shape, F32)

    def scores(c, buf):
        k0 = pl.multiple_of(c * tk, tk)
        for mm in range(2):
            kt = kb_ref[pl.ds(k0, tk), mm * HEAD_DIM:(mm + 1) * HEAD_DIM]
            s_ref[buf, mm] = jnp.dot(kt, qt_ref[mm * HEAD_DIM:(mm + 1) * HEAD_DIM, :],
                                     preferred_element_type=F32)

    def consume(c, buf, bias):
        ps, alphas = [], []
        for mm in range(2):
            s = s_ref[buf, mm]
            if bias is not None:
                s = s + bias
            m_prev = m_ref[mm]
            m_new = jnp.maximum(m_prev, jnp.max(s, axis=0, keepdims=True))
            alpha = jnp.exp2(m_prev - m_new)
            p = jnp.exp2(s - m_new)
            l_ref[mm] = alpha * l_ref[mm] + jnp.sum(p, axis=0, keepdims=True)
            m_ref[mm] = m_new
            ps.append(p.astype(BF16))
            alphas.append(alpha)
        for mm in range(2):
            pv = jnp.dot(vt_ref[c], ps[mm], preferred_element_type=F32)
            acc_ref[mm] = alphas[mm] * acc_ref[mm] + pv

    scores(0, 0)

    def far_pair(t, carry):
        scores(2 * t + 1, 1)
        consume(2 * t, 0, None)
        scores(2 * t + 2, 0)
        consume(2 * t + 1, 1, None)
        return carry

    def far_quad(t, carry):
        far_pair(2 * t, carry)
        far_pair(2 * t + 1, carry)
        return carry

    n_far = jnp.maximum(i - 1, 0)
    n_quad = n_far // 4
    lax.fori_loop(0, n_quad, far_quad, 0)
    lax.fori_loop(2 * n_quad, n_far // 2, far_pair, 0)

    @pl.when(i == 0)
    def _():
        consume(0, 0, bias_ref[1])

    @pl.when(i % 2 == 1)
    def _():
        scores(i, 1)
        consume(i - 1, 0, bias_ref[0])
        consume(i, 1, bias_ref[1])

    @pl.when((i % 2 == 0) & (i > 0))
    def _():
        scores(i - 1, 1)
        consume(i - 2, 0, None)
        scores(i, 0)
        consume(i - 1, 1, bias_ref[0])
        consume(i, 0, bias_ref[1])

    lam = lam_ref[0]
    o = acc_ref[0] / l_ref[0] - lam * (acc_ref[1] / l_ref[1])
    ms = jnp.mean(o * o, axis=0, keepdims=True)
    on = ((o * lax.rsqrt(ms + EPS)) * sw_ref[...]) * (1.0 - lam_init)
    o_ref[...] = on.T.astype(o_ref.dtype)


SATTN_CP = 8
SATTN_NBUF = 4
ROWS_PER_PAGE = PAGE_SIZE * N_HEADS


def _sattn_step(step_id, cps, pt_ref, lam_ref, q_ref, kn_ref, vn_ref, hm_ref, blast_ref, bnew_ref, sw_ref,
                ck_ref, cv_ref, o_ref, kbuf, vbuf, ksem, vsem, sm_ref, sl_ref, sacc_ref,
                n_req, n_chunks, lam_init):
    spr = n_chunks // cps
    r = step_id // spr
    c0 = (step_id % spr) * cps
    cp = SATTN_CP
    nbuf = SATTN_NBUF
    rows = cp * ROWS_PER_PAGE

    def page_copies(req, chunk, slot):
        out = []
        for p in range(cp):
            pg = pt_ref[req, chunk * cp + p]
            out.append(pltpu.make_async_copy(ck_ref.at[0, pg], kbuf.at[slot, p], ksem.at[slot]))
            out.append(pltpu.make_async_copy(cv_ref.at[0, pg], vbuf.at[slot, p], vsem.at[slot]))
        return out

    def start_chunk(g):
        req = g // n_chunks
        chunk = g % n_chunks
        for c in page_copies(req, chunk, g % nbuf):
            c.start()

    @pl.when(step_id == 0)
    def _():
        for g in range(nbuf - 1):
            start_chunk(g)

    @pl.when(c0 == 0)
    def _():
        sm_ref[...] = jnp.full(sm_ref.shape, -jnp.inf, F32)
        sl_ref[...] = jnp.zeros(sl_ref.shape, F32)
        sacc_ref[...] = jnp.zeros(sacc_ref.shape, F32)

    qs = q_ref[0]
    ds = qs.shape[0]
    lane = lax.broadcasted_iota(jnp.int32, (ds, V_DIM), 1)
    pieces = []
    for h in range(N_HEADS):
        qh = qs[:, h * V_DIM:(h + 1) * V_DIM]
        for mm in range(2):
            keep = (lane < HEAD_DIM) if mm == 0 else (lane >= HEAD_DIM)
            pieces.append(jnp.where(keep, qh, 0.0))
    q2 = jnp.concatenate(pieces, axis=0).astype(BF16)
    nrow = q2.shape[0]

    def step(s, state, v2):
        m_prev, l_prev, acc = state
        m_new = jnp.maximum(m_prev, jnp.max(s, axis=1, keepdims=True))
        alpha = jnp.exp2(m_prev - m_new)
        p = jnp.exp2(s - m_new)
        l_new = alpha * l_prev + jnp.sum(p, axis=1, keepdims=True)
        acc_new = alpha * acc + jnp.dot(p.astype(BF16), v2, preferred_element_type=F32)
        return m_new, l_new, acc_new

    nt = (((1,), (1,)), ((), ()))
    hm = hm_ref[...]
    hm_chunk = jnp.concatenate([hm] * cp, axis=1)
    last_mask = jnp.concatenate([hm] * (cp - 1) + [blast_ref[...]], axis=1)

    def chunk_body(c, state):
        g = r * n_chunks + c
        nxt = g + (nbuf - 1)

        @pl.when(nxt < n_req * n_chunks)
        def _():
            start_chunk(nxt)

        slot = g % nbuf
        for cpy in page_copies(r, c, slot):
            cpy.wait()
        k2 = kbuf[slot].reshape(rows, V_DIM).astype(BF16)
        v2 = vbuf[slot].reshape(rows, V_DIM).astype(BF16)
        s = lax.dot_general(q2, k2, nt, preferred_element_type=F32)
        s = s + jnp.where(c == n_chunks - 1, last_mask, hm_chunk)
        return step(s, state, v2)

    state = lax.fori_loop(c0, c0 + cps, chunk_body, (sm_ref[...], sl_ref[...], sacc_ref[...]))
    sm_ref[...], sl_ref[...], sacc_ref[...] = state

    @pl.when(c0 + cps == n_chunks)
    def _():
        kn = kn_ref[0].astype(BF16)
        vn = vn_ref[0].astype(BF16)
        s = lax.dot_general(q2, kn, nt, preferred_element_type=F32) + bnew_ref[...]
        _, l_fin, acc = step(s, state, vn)

        lam = lam_ref[0]
        o = acc / l_fin
        outs = []
        for h in range(N_HEADS):
            o1 = o[(2 * h) * ds:(2 * h + 1) * ds]
            o2 = o[(2 * h + 1) * ds:(2 * h + 2) * ds]
            oh = o1 - lam * o2
            ms = jnp.mean(oh * oh, axis=-1, keepdims=True)
            outs.append(((oh * lax.rsqrt(ms + EPS)) * sw_ref[...]) * (1.0 - lam_init))
        o_ref[0] = jnp.concatenate(outs, axis=1)


def _attn_kernel(pt_ref, lam_ref,
                 qt_ref, kb_ref, vt_ref, bias_ref, swc_ref,
                 q_ref, kn_ref, vn_ref, hm_ref, blast_ref, bnew_ref, swr_ref, ck_ref, cv_ref,
                 op_ref, os_ref,
                 m_ref, l_ref, acc_ref, s_ref, kbuf, vbuf, ksem, vsem, sm_ref, sl_ref, sacc_ref,
                 *, nq, cps, n_req, n_chunks, lam_init):
    step_id = pl.program_id(0)
    _pattn_step(step_id % nq, lam_ref, qt_ref, kb_ref, vt_ref, bias_ref, swc_ref, op_ref,
                m_ref, l_ref, acc_ref, s_ref, lam_init)
    _sattn_step(step_id, cps, pt_ref, lam_ref, q_ref, kn_ref, vn_ref, hm_ref, blast_ref, bnew_ref, swr_ref,
                ck_ref, cv_ref, os_ref, kbuf, vbuf, ksem, vsem, sm_ref, sl_ref, sacc_ref,
                n_req, n_chunks, lam_init)


def _attn(page_table, lam, qt, kb, vt, bias_tiles, swc, batch, seq,
          q3, kn3, vn3, hm, blast, bnew, swr, cache_k, cache_v, lam_init):
    tq = PATTN_TQ
    tk = KV_TILE
    assert seq % tq == 0 and tq == tk
    nq = seq // tq
    nkv = seq // tk
    steps = batch * N_HEADS * nq
    n_req, ds, _ = q3.shape
    n_pages = page_table.shape[1]
    assert n_pages % SATTN_CP == 0
    n_chunks = n_pages // SATTN_CP
    assert (n_req * n_chunks) % steps == 0, "sample page chunks must split evenly over the prompt grid steps"
    cps = n_req * n_chunks // steps
    assert n_chunks % cps == 0, "a grid step must stay inside one sample request"
    spr = n_chunks // cps
    nrow = 2 * N_HEADS * ds
    hq = N_HEADS * nq
    kernel = functools.partial(_attn_kernel, nq=nq, cps=cps, n_req=n_req, n_chunks=n_chunks, lam_init=lam_init)
    const = lambda shape: pl.BlockSpec(shape, lambda s, pt: (0,) * len(shape))
    req_block = lambda shape: pl.BlockSpec(shape, lambda s, pt: (s // spr, 0, 0))
    return pl.pallas_call(
        kernel,
        grid_spec=pltpu.PrefetchScalarGridSpec(
            num_scalar_prefetch=1,
            grid=(steps,),
            in_specs=[
                pl.BlockSpec(memory_space=pltpu.SMEM),
                pl.BlockSpec((V_DIM, tq), lambda s, pt: ((s // nq) % N_HEADS, (s // hq) * nq + s % nq)),
                pl.BlockSpec((seq, V_DIM), lambda s, pt: (s // hq, (s // nq) % N_HEADS)),
                pl.BlockSpec((nkv, V_DIM, tk), lambda s, pt: (s // hq, (s // nq) % N_HEADS, 0)),
                pl.BlockSpec((None, 1 + tq // tk, tk, tq), lambda s, pt: ((s // nq) % N_HEADS, 0, 0, 0)),
                const(swc.shape),
                req_block((1, ds, ATTN_WIDTH)),
                req_block((1, ds * N_HEADS, V_DIM)),
                req_block((1, ds * N_HEADS, V_DIM)),
                const(hm.shape), const(blast.shape), const(bnew.shape), const(swr.shape),
                pl.BlockSpec(memory_space=pl.ANY),
                pl.BlockSpec(memory_space=pl.ANY),
            ],
            out_specs=[
                pl.BlockSpec((tq, V_DIM), lambda s, pt: ((s // hq) * nq + s % nq, (s // nq) % N_HEADS)),
                req_block((1, ds, ATTN_WIDTH)),
            ],
            scratch_shapes=[
                pltpu.VMEM((2, 1, tq), F32),
                pltpu.VMEM((2, 1, tq), F32),
                pltpu.VMEM((2, V_DIM, tq), F32),
                pltpu.VMEM((2, 2, tk, tq), F32),
                pltpu.VMEM((SATTN_NBUF, SATTN_CP, PAGE_SIZE, N_HEADS, V_DIM), F32),
                pltpu.VMEM((SATTN_NBUF, SATTN_CP, PAGE_SIZE, N_HEADS, V_DIM), F32),
                pltpu.SemaphoreType.DMA((SATTN_NBUF,)),
                pltpu.SemaphoreType.DMA((SATTN_NBUF,)),
                pltpu.VMEM((nrow, 1), F32),
                pltpu.VMEM((nrow, 1), F32),
                pltpu.VMEM((nrow, V_DIM), F32),
            ],
        ),
        out_shape=[
            jax.ShapeDtypeStruct((batch * seq, ATTN_WIDTH), BF16),
            jax.ShapeDtypeStruct((n_req, ds, ATTN_WIDTH), F32),
        ],
        compiler_params=_cparams(("arbitrary",)),
        name="attn",
    )(page_table, lam, qt, kb, vt, bias_tiles, swc, q3, kn3, vn3, hm, blast, bnew, swr, cache_k, cache_v)


MIX_TM = 512
ROUTE_ROWS = 8
LANES = 128
SUBLANES = 8


def _store_row_tiles(ref, x):
    n, d = x.shape
    c = d // LANES
    for j in range(c):
        ref[pl.ds(j, n, stride=c), :] = x[:, j * LANES:(j + 1) * LANES]


def _load_row_tiles(ref, n, c):
    return jnp.concatenate([ref[pl.ds(j, n, stride=c), :] for j in range(c)], axis=1)


def _mix_kernel(a_ref, c_ref, x_ref, wo_ref, n2_ref, wr_ref, br_ref, tri_ref, cin_ref,
                x2_ref, h_ref, route_ref, cnt_ref):
    i = pl.program_id(0)
    tm = x_ref.shape[0]
    half = a_ref.shape[1]

    @pl.when(i == 0)
    def _():
        cnt_ref[...] = cin_ref[...]

    x2 = (x_ref[...]
          + jnp.dot(a_ref[...].astype(BF16), wo_ref[0:half, :], preferred_element_type=F32)
          + jnp.dot(c_ref[...].astype(BF16), wo_ref[half:, :], preferred_element_type=F32))
    x2_ref[...] = x2
    ms = jnp.mean(x2 * x2, axis=-1, keepdims=True)
    h = (x2 * lax.rsqrt(ms + EPS)) * n2_ref[...]
    _store_row_tiles(h_ref, h)

    h_hi = h.astype(BF16)
    h_lo = (h - h_hi.astype(F32)).astype(BF16)
    wr = wr_ref[...]
    w_hi = wr.astype(BF16)
    w_lo = (wr - w_hi.astype(F32)).astype(BF16)
    nt = (((1,), (1,)), ((), ()))
    logits = (lax.dot_general(w_hi, h_hi, nt, preferred_element_type=F32)
              + lax.dot_general(w_hi, h_lo, nt, preferred_element_type=F32)
              + lax.dot_general(w_lo, h_hi, nt, preferred_element_type=F32)) + br_ref[...]
    gl = logits[0:N_GROUPS]
    el = logits[N_GROUPS:N_GROUPS + N_EXPERTS]

    sub8 = lax.broadcasted_iota(jnp.int32, (N_GROUPS, tm), 0)

    def first_argmax(vals):
        top = jnp.max(vals, axis=0, keepdims=True)
        idx = jnp.min(jnp.where(vals == top, sub8, N_GROUPS), axis=0, keepdims=True)
        return top, idx

    gexp = jnp.exp(gl - jnp.max(gl, axis=0, keepdims=True))
    gp = gexp / jnp.sum(gexp, axis=0, keepdims=True)
    g_prob, g_idx = first_argmax(gp)

    el3 = el.reshape(N_GROUPS, EXPERTS_PER_GROUP, tm)
    sel = jnp.zeros((EXPERTS_PER_GROUP, tm), F32)
    for g in range(N_GROUPS):
        sel = sel + jnp.where(g_idx == g, el3[g], 0.0)
    v1, i1 = first_argmax(sel)
    v2, i2 = first_argmax(jnp.where(sub8 == i1, -jnp.inf, sel))
    e2 = jnp.exp(v2 - v1)
    p1 = 1.0 / (1.0 + e2)
    p2 = e2 / (1.0 + e2)
    ea = g_idx * EXPERTS_PER_GROUP + i1
    eb = g_idx * EXPERTS_PER_GROUP + i2

    sub64 = lax.broadcasted_iota(jnp.int32, (N_EXPERTS, tm), 0)
    oh_a = sub64 == ea
    oh_b = sub64 == eb
    oh = jnp.where(oh_a | oh_b, 1.0, 0.0)
    before = jnp.dot(oh.astype(BF16), tri_ref[...], preferred_element_type=F32) + cnt_ref[...]
    rank_a = jnp.sum(jnp.where(oh_a, before, 0.0), axis=0, keepdims=True)
    rank_b = jnp.sum(jnp.where(oh_b, before, 0.0), axis=0, keepdims=True)
    cnt_ref[...] = cnt_ref[...] + jnp.sum(oh, axis=1, keepdims=True)

    route_ref[...] = jnp.concatenate(
        [ea.astype(F32), eb.astype(F32), rank_a, rank_b, g_prob * p1, g_prob * p2,
         jnp.zeros((ROUTE_ROWS - 6, tm), F32)], axis=0)


def _mix(a, c, x2d, wo_b, n2, wr, br, tri, cnt_in):
    t, d = x2d.shape
    tm = MIX_TM
    assert t % tm == 0
    const = lambda shape: pl.BlockSpec(shape, lambda i: (0,) * len(shape))
    return pl.pallas_call(
        _mix_kernel,
        grid=(t // tm,),
        in_specs=[
            pl.BlockSpec((tm, a.shape[1]), lambda i: (i, 0)),
            pl.BlockSpec((tm, c.shape[1]), lambda i: (i, 0)),
            pl.BlockSpec((tm, d), lambda i: (i, 0)),
            const(wo_b.shape), const(n2.shape), const(wr.shape), const(br.shape), const(tri.shape),
            const(cnt_in.shape),
        ],
        out_specs=[
            pl.BlockSpec((tm, d), lambda i: (i, 0)),
            pl.BlockSpec((tm * (d // LANES), LANES), lambda i: (i, 0)),
            pl.BlockSpec((ROUTE_ROWS, tm), lambda i: (0, i)),
            const(cnt_in.shape),
        ],
        out_shape=[
            jax.ShapeDtypeStruct((t, d), F32),
            jax.ShapeDtypeStruct((t * (d // LANES), LANES), F32),
            jax.ShapeDtypeStruct((ROUTE_ROWS, t), F32),
            jax.ShapeDtypeStruct(cnt_in.shape, F32),
        ],
        compiler_params=_cparams(("arbitrary",)),
        name="mix",
    )(a, c, x2d, wo_b, n2, wr, br, tri, cnt_in)


ROW_TM = 256
ROW_UNROLL = 8
EXPERT_BM = 256


def _row_tile(row):
    return pl.ds(pl.multiple_of(row * SUBLANES, SUBLANES), SUBLANES)


def _dispatch_kernel(slots_ref, h_ref, xin_ref, xb_ref, sem):
    del xin_ref
    tm = slots_ref.shape[2]

    def issue(t, carry):
        for j in range(2):
            s = slots_ref[0, j, t]
            pltpu.make_async_copy(h_ref.at[_row_tile(t)], xb_ref.at[_row_tile(s)], sem).start(priority=j)
        return carry

    lax.fori_loop(0, tm, issue, 0, unroll=ROW_UNROLL)

    def drain(t, carry):
        for j in range(2):
            pltpu.make_async_copy(h_ref.at[_row_tile(0)], xb_ref.at[_row_tile(0)], sem).wait()
        return carry

    lax.fori_loop(0, tm, drain, 0, unroll=ROW_UNROLL)


def _dispatch(slots3, h, xb):
    tm = ROW_TM
    assert h.shape[0] % (tm * SUBLANES) == 0 and h.shape[1] == LANES
    return pl.pallas_call(
        _dispatch_kernel,
        grid=(h.shape[0] // (tm * SUBLANES),),
        in_specs=[
            pl.BlockSpec((1, 2, tm), lambda i: (i, 0, 0), memory_space=pltpu.SMEM),
            pl.BlockSpec((tm * SUBLANES, LANES), lambda i: (i, 0)),
            pl.BlockSpec(memory_space=pl.ANY),
        ],
        out_specs=pl.BlockSpec(memory_space=pl.ANY),
        out_shape=jax.ShapeDtypeStruct(xb.shape, xb.dtype),
        scratch_shapes=[pltpu.SemaphoreType.DMA],
        input_output_aliases={2: 0},
        compiler_params=_cparams(("arbitrary",)),
        name="dispatch",
    )(slots3, h, xb)


def _experts_kernel(be_ref, nb_ref, xb_ref, w1_ref, w3_ref, w2_ref, yb_ref, w1b, w3b, w2b):
    i = pl.program_id(0)
    bm = EXPERT_BM

    @pl.when((i == 0) | (be_ref[i] != be_ref[jnp.maximum(i - 1, 0)]))
    def _():
        w1b[...] = w1_ref[0].astype(BF16)
        w3b[...] = w3_ref[0].astype(BF16)
        w2b[...] = w2_ref[0].astype(BF16)

    @pl.when(i < nb_ref[0])
    def _():
        x = _load_row_tiles(xb_ref, bm, SUBLANES).astype(BF16)
        a = jnp.dot(x, w1b[...], preferred_element_type=F32)
        b = jnp.dot(x, w3b[...], preferred_element_type=F32)
        mid = (a * (1.0 / (1.0 + jnp.exp(-a)))) * b
        _store_row_tiles(yb_ref, jnp.dot(mid.astype(BF16), w2b[...], preferred_element_type=F32))

    @pl.when(i >= nb_ref[0])
    def _():
        yb_ref[...] = jnp.zeros(yb_ref.shape, F32)


def _experts(block_expert, n_used, xb, w1, w3, w2):
    bm = EXPERT_BM
    _, d, hid = w1.shape
    rows = bm * SUBLANES
    return pl.pallas_call(
        _experts_kernel,
        grid_spec=pltpu.PrefetchScalarGridSpec(
            num_scalar_prefetch=2,
            grid=(xb.shape[0] // rows,),
            in_specs=[
                pl.BlockSpec((rows, LANES), lambda i, be, nb: (i, 0)),
                pl.BlockSpec((1, d, hid), lambda i, be, nb: (be[i], 0, 0)),
                pl.BlockSpec((1, d, hid), lambda i, be, nb: (be[i], 0, 0)),
                pl.BlockSpec((1, hid, d), lambda i, be, nb: (be[i], 0, 0)),
            ],
            out_specs=pl.BlockSpec((rows, LANES), lambda i, be, nb: (i, 0)),
            scratch_shapes=[
                pltpu.VMEM((d, hid), BF16),
                pltpu.VMEM((d, hid), BF16),
                pltpu.VMEM((hid, d), BF16),
            ],
        ),
        out_shape=jax.ShapeDtypeStruct(xb.shape, F32),
        compiler_params=_cparams(("arbitrary",)),
        name="experts",
    )(block_expert, n_used, xb, w1, w3, w2)


def _combine_kernel(slots_ref, x2_ref, g_ref, yb_ref, o_ref, ybuf, sem):
    tm = x2_ref.shape[0]

    def issue(t, carry):
        for j in range(2):
            s = slots_ref[0, j, t]
            pltpu.make_async_copy(yb_ref.at[_row_tile(s)], ybuf.at[j, _row_tile(t)], sem).start(priority=j)
        return carry

    lax.fori_loop(0, tm, issue, 0, unroll=ROW_UNROLL)

    def drain(t, carry):
        for j in range(2):
            pltpu.make_async_copy(yb_ref.at[_row_tile(0)], ybuf.at[j, _row_tile(0)], sem).wait()
        return carry

    lax.fori_loop(0, tm, drain, 0, unroll=ROW_UNROLL)
    g = g_ref[...]
    ya = _load_row_tiles(ybuf.at[0], tm, SUBLANES)
    yb = _load_row_tiles(ybuf.at[1], tm, SUBLANES)
    o_ref[...] = x2_ref[...] + (g[:, 0:1] * ya + g[:, 1:2] * yb)


def _combine(slots3, x2, gates, yb):
    t, d = x2.shape
    tm = ROW_TM
    assert t % tm == 0
    return pl.pallas_call(
        _combine_kernel,
        grid=(t // tm,),
        in_specs=[
            pl.BlockSpec((1, 2, tm), lambda i: (i, 0, 0), memory_space=pltpu.SMEM),
            pl.BlockSpec((tm, d), lambda i: (i, 0)),
            pl.BlockSpec((tm, 2), lambda i: (i, 0)),
            pl.BlockSpec(memory_space=pl.ANY),
        ],
        out_specs=pl.BlockSpec((tm, d), lambda i: (i, 0)),
        out_shape=jax.ShapeDtypeStruct((t, d), F32),
        scratch_shapes=[pltpu.VMEM((2, tm * SUBLANES, LANES), F32), pltpu.SemaphoreType.DMA],
        compiler_params=_cparams(("arbitrary",)),
        name="combine",
    )(slots3, x2, gates, yb)


def _t5_bucket(dist):
    n = np.maximum(dist, 0)
    max_exact = N_BUCKETS // 2
    nf = np.maximum(n, 1).astype(np.float32)
    large = max_exact + (np.log(nf / np.float32(max_exact)) / np.float32(math.log(MAX_DISTANCE / max_exact))
                         * np.float32(N_BUCKETS - max_exact)).astype(np.int32)
    large = np.minimum(large, N_BUCKETS - 1)
    return np.where(n < max_exact, n, large)


def _bias_table(rel_bias, d_lo, d_hi):
    dist = np.arange(d_lo, d_hi)
    onehot = (_t5_bucket(dist)[None, :] == np.arange(N_BUCKETS)[:, None]).astype(np.float32)
    tab = jnp.sum(rel_bias.T[:, :, None] * onehot[None], axis=1)
    tab = (tab - rel_bias[N_BUCKETS - 1][:, None]) * LOG2E
    return jnp.where(jnp.asarray(dist >= 0)[None, :], tab, MASKED)


def _toeplitz(v, nr, nc):
    n = nr + nc - 1
    assert v.shape[1] == n
    w = jnp.concatenate([v, jnp.zeros((v.shape[0], 1), v.dtype)], axis=1)
    flat = jnp.tile(w, (1, nr))[:, :nr * n]
    return flat.reshape(v.shape[0], nr, n)[:, :, nr - 1:nr - 1 + nc]


def _layer(xp2, xs2, batch, seq, db, ds, cache_k, cache_v, state_conv, page_table, rel_bias, lam, lam_init, p):
    d = xp2.shape[1]
    w = ATTN_WIDTH
    qwc = (jnp.tile(p["q_norm_w"], 2 * N_HEADS) * (HEAD_DIM ** -0.5 * LOG2E)).reshape(w, 1)
    kw = jnp.tile(p["k_norm_w"], 2 * N_HEADS).reshape(1, w)
    gid = np.arange(w) // HEAD_DIM
    gmat = jnp.asarray((gid[:, None] == gid[None, :]).astype(np.float32) / HEAD_DIM, BF16)
    n1 = p["norm1_w"].reshape(1, d)
    win_b = p["w_in"].astype(BF16)
    lam1 = lam.reshape(1).astype(F32)

    qt_p, kb_p, k32_p, vt_p, v32_p, u_p = _proj(xp2, n1, win_b, qwc, kw, gmat)
    qt_s, _, k32_s, _, v32_s, u_s = _proj(xs2, n1, win_b, qwc, kw, gmat)

    cw = p["conv_w"]
    cb = p["conv_b"].reshape(1, CONV_DIM)
    lnw = p["conv_ln_w"].reshape(1, CONV_DIM)
    lnb = p["conv_ln_b"].reshape(1, CONV_DIM)
    oc_p = _pconv(u_p, batch, seq, cw, cb, lnw, lnb)
    oc_s, conv_s = _sconv(state_conv, u_s.reshape(db, ds, CONV_DIM), cw, cb, lnw, lnb)
    conv_p = u_p.reshape(batch, seq, CONV_DIM)[:, seq - (CONV_W - 1):]

    tq, tk = PATTN_TQ, KV_TILE
    bias_tiles = jnp.stack(
        [_toeplitz(_bias_table(rel_bias, (1 - c) * tk - (tk - 1), (1 - c) * tk + tq), tk, tq)
         for c in range(1 + tq // tk)], axis=1)
    swc = p["subln_w"].reshape(V_DIM, 1)

    nrow = 2 * N_HEADS * ds
    eye = jnp.asarray(np.eye(N_HEADS, dtype=bool))

    def per_head_rows(t3):
        full = jnp.where(eye[:, None, None, None, :], t3[:, None, :, :, None], MASKED)
        full = jnp.broadcast_to(full, (N_HEADS, 2, ds, t3.shape[2], N_HEADS))
        return full.reshape(nrow, t3.shape[2] * N_HEADS)

    hm = per_head_rows(jnp.zeros((N_HEADS, ds, PAGE_SIZE), F32))
    blast = per_head_rows(_toeplitz(_bias_table(rel_bias, 1, PAGE_SIZE + ds)[:, ::-1], ds, PAGE_SIZE))
    bnew = per_head_rows(_toeplitz(_bias_table(rel_bias, -(ds - 1), ds)[:, ::-1], ds, ds))
    swr = p["subln_w"].reshape(1, V_DIM)
    q_s = (qt_s.T.astype(F32)).reshape(db, ds, w)
    kn3 = k32_s.reshape(db, ds * N_HEADS, V_DIM)
    vn3 = v32_s.reshape(db, ds * N_HEADS, V_DIM)
    a_p, a_s = _attn(page_table, lam1, qt_p, kb_p, vt_p, bias_tiles, swc, batch, seq,
                     q_s, kn3, vn3, hm, blast, bnew, swr, cache_k, cache_v, lam_init)

    wo_b = p["w_out"].astype(BF16)
    n2 = p["norm2_w"].reshape(1, d)
    wr = jnp.concatenate([p["w_group"], p["w_expert"]], axis=1).T
    br = jnp.concatenate([p["b_group"], p["b_expert"]]).reshape(-1, 1)
    tri = jnp.asarray(np.triu(np.ones((MIX_TM, MIX_TM), np.float32), 1), BF16)
    cnt0 = jnp.zeros((N_EXPERTS, 1), F32)
    x2_p, h_p, route_p, cnt_p = _mix(a_p, oc_p, xp2, wo_b, n2, wr, br, tri, cnt0)
    x2_s, h_s, route_s, cnt = _mix(a_s.reshape(db * ds, w), oc_s.reshape(db * ds, CONV_DIM), xs2,
                                   wo_b, n2, wr, br, tri, cnt_p)

    bm = EXPERT_BM
    counts = cnt[:, 0].astype(jnp.int32)
    padded = (counts + bm - 1) // bm * bm
    pad_end = jnp.cumsum(padded)
    pad_start = pad_end - padded
    n_assign = 2 * (xp2.shape[0] + xs2.shape[0])
    n_blocks = -(-n_assign // bm) + N_EXPERTS
    nslot = n_blocks * bm
    block_first = jnp.arange(n_blocks, dtype=jnp.int32) * bm
    block_expert = jnp.minimum(
        jnp.sum((pad_end[None, :] <= block_first[:, None]).astype(jnp.int32), axis=1), N_EXPERTS - 1)
    n_used = (pad_end[-1] // bm).reshape(1).astype(jnp.int32)
    expert_ids = jnp.arange(N_EXPERTS, dtype=jnp.int32)[:, None, None]

    def slots_of(route):
        e = route[0:2].astype(jnp.int32)
        rank = route[2:4].astype(jnp.int32)
        s = jnp.sum(jnp.where(e[None] == expert_ids, pad_start[:, None, None], 0), axis=0) + rank
        t = s.shape[1]
        return s.reshape(2, t // ROW_TM, ROW_TM).transpose(1, 0, 2)

    slots_p, slots_s = slots_of(route_p), slots_of(route_s)

    assert d == SUBLANES * LANES, "a model row must be exactly one 8x128 tile for the row-tile DMAs"
    xb = jnp.zeros((nslot * SUBLANES, LANES), F32)
    xb = _dispatch(slots_p, h_p, xb)
    xb = _dispatch(slots_s, h_s, xb)
    yb = _experts(block_expert, n_used, xb, p["w1"], p["w3"], p["w2"])
    y_p = _combine(slots_p, x2_p, route_p[4:6].T, yb)
    y_s = _combine(slots_s, x2_s, route_s[4:6].T, yb)

    kshape = (N_HEADS, V_DIM)
    return (y_p, y_s,
            k32_p.reshape(batch, seq, *kshape), v32_p.reshape(batch, seq, *kshape), conv_p,
            k32_s.reshape(db, ds, *kshape), v32_s.reshape(db, ds, *kshape), conv_s)


def kernel(x_prompt, x_sample, cache_k, cache_v, state_conv, page_table, rel_bias, norm1_w, w_in, q_norm_w, k_norm_w, lambda_q1, lambda_k1, lambda_q2, lambda_k2, subln_w, conv_w, conv_b, conv_ln_w, conv_ln_b, w_out, norm2_w, w_group, b_group, w_expert, b_expert, w1, w3, w2):
    batch, seq, d = x_prompt.shape
    db, ds, _ = x_sample.shape
    depth = w_in.shape[0]
    assert depth == 1, "paged caches are read once; a deeper trunk needs one cache slice per layer"
    xp = x_prompt.reshape(batch * seq, d)
    xs = x_sample.reshape(db * ds, d)
    l = 0
    lam_init = 0.8 - 0.6 * math.exp(-0.3 * l)
    lam = (jnp.exp(jnp.sum(lambda_q1[l].astype(F32) * lambda_k1[l].astype(F32)))
           - jnp.exp(jnp.sum(lambda_q2[l].astype(F32) * lambda_k2[l].astype(F32)))
           + lam_init)
    params = dict(norm1_w=norm1_w[l], w_in=w_in[l], q_norm_w=q_norm_w[l], k_norm_w=k_norm_w[l],
                  subln_w=subln_w[l], conv_w=conv_w[l], conv_b=conv_b[l], conv_ln_w=conv_ln_w[l],
                  conv_ln_b=conv_ln_b[l], w_out=w_out[l], norm2_w=norm2_w[l], w_group=w_group[l],
                  b_group=b_group[l], w_expert=w_expert[l], b_expert=b_expert[l],
                  w1=w1[l], w3=w3[l], w2=w2[l])
    y_p, y_s, k_p, v_p, c_p, k_s, v_s, c_s = _layer(
        xp, xs, batch, seq, db, ds, cache_k, cache_v, state_conv[l], page_table, rel_bias, lam, lam_init, params)
    return (y_p.reshape(batch, seq, d), y_s.reshape(db, ds, d),
            k_p[None], v_p[None], c_p[None], k_s[None], v_s[None], c_s[None])
```

```python
import functools
import math

import numpy as np
import jax
import jax.numpy as jnp
from jax import lax
from jax.experimental import pallas as pl
from jax.experimental.pallas import tpu as pltpu

F32 = jnp.float32
BF16 = jnp.bfloat16

N_HEADS = 4
HEAD_DIM = 64
V_DIM = 2 * HEAD_DIM
ATTN_WIDTH = N_HEADS * V_DIM
CONV_DIM = 512
CONV_W = 31
N_BUCKETS = 32
MAX_DISTANCE = 128
N_GROUPS = 8
EXPERTS_PER_GROUP = 8
N_EXPERTS = N_GROUPS * EXPERTS_PER_GROUP
PAGE_SIZE = 128
EPS = 1e-6
MASKED = -1e30

VMEM_LIMIT = 56 * 1024 * 1024


def _cparams(sem):
    return pltpu.CompilerParams(dimension_semantics=sem, vmem_limit_bytes=VMEM_LIMIT)


PROJ_TM = 512
KV_TILE = 512
LOG2E = math.log2(math.e)


def _proj_kernel(x_ref, n1_ref, win_ref, qwc_ref, kw_ref, g_ref,
                 qt_ref, kb_ref, k32_ref, vt_ref, v32_ref, u_ref):
    tm = x_ref.shape[0]
    x = x_ref[...]
    ms = jnp.mean(x * x, axis=-1, keepdims=True)
    xn = (x * lax.rsqrt(ms + EPS)) * n1_ref[...]
    z = jnp.dot(xn.astype(BF16), win_ref[...], preferred_element_type=F32)
    w = ATTN_WIDTH
    zq, zk, zv = z[:, :w], z[:, w:2 * w], z[:, 2 * w:3 * w]
    g1, g2 = z[:, 3 * w:3 * w + CONV_DIM], z[:, 3 * w + CONV_DIM:]

    kk = zk * zk
    hi = kk.astype(BF16)
    lo = (kk - hi.astype(F32)).astype(BF16)
    kms = (jnp.dot(hi, g_ref[...], preferred_element_type=F32)
           + jnp.dot(lo, g_ref[...], preferred_element_type=F32))
    kn = (zk * lax.rsqrt(kms + EPS)) * kw_ref[...]
    kb_ref[...] = kn.astype(BF16)
    for h in range(N_HEADS):
        k32_ref[pl.ds(h, tm, stride=N_HEADS), :] = kn[:, h * V_DIM:(h + 1) * V_DIM]
        v32_ref[pl.ds(h, tm, stride=N_HEADS), :] = zv[:, h * V_DIM:(h + 1) * V_DIM]
    zvt = zv.T
    for t in range(tm // KV_TILE):
        vt_ref[t] = zvt[:, t * KV_TILE:(t + 1) * KV_TILE].astype(BF16)

    q3 = zq.T.reshape(2 * N_HEADS, HEAD_DIM, tm)
    qms = jnp.mean(q3 * q3, axis=1, keepdims=True)
    qn = (q3 * lax.rsqrt(qms + EPS)).reshape(w, tm) * qwc_ref[...]
    qt_ref[...] = qn.astype(BF16)

    u_ref[...] = g1 * (1.0 / (1.0 + jnp.exp(-g2)))


def _proj(x2d, n1, win_b, qwc, kw, gmat):
    t = x2d.shape[0]
    tm = PROJ_TM
    assert t % tm == 0
    w = ATTN_WIDTH
    const = lambda shape: pl.BlockSpec(shape, lambda i: (0,) * len(shape))
    return pl.pallas_call(
        _proj_kernel,
        grid=(t // tm,),
        in_specs=[
            pl.BlockSpec((tm, x2d.shape[1]), lambda i: (i, 0)),
            const(n1.shape), const(win_b.shape), const(qwc.shape), const(kw.shape), const(gmat.shape),
        ],
        out_specs=[
            pl.BlockSpec((w, tm), lambda i: (0, i)),
            pl.BlockSpec((tm, w), lambda i: (i, 0)),
            pl.BlockSpec((tm * N_HEADS, V_DIM), lambda i: (i, 0)),
            pl.BlockSpec((tm // KV_TILE, w, KV_TILE), lambda i: (i, 0, 0)),
            pl.BlockSpec((tm * N_HEADS, V_DIM), lambda i: (i, 0)),
            pl.BlockSpec((tm, CONV_DIM), lambda i: (i, 0)),
        ],
        out_shape=[
            jax.ShapeDtypeStruct((w, t), BF16),
            jax.ShapeDtypeStruct((t, w), BF16),
            jax.ShapeDtypeStruct((t * N_HEADS, V_DIM), F32),
            jax.ShapeDtypeStruct((t // KV_TILE, w, KV_TILE), BF16),
            jax.ShapeDtypeStruct((t * N_HEADS, V_DIM), F32),
            jax.ShapeDtypeStruct((t, CONV_DIM), F32),
        ],
        compiler_params=_cparams(("arbitrary",)),
        name="proj",
    )(x2d, n1, win_b, qwc, kw, gmat)


CONV_TC = 512
CONV_RC = 64
CONV_HALO = 32


def _ln_swish(c, lnw, lnb):
    mu = jnp.mean(c, axis=-1, keepdims=True)
    d = c - mu
    var = jnp.mean(d * d, axis=-1, keepdims=True)
    cn = (d * lax.rsqrt(var + EPS)) * lnw + lnb
    return cn * (1.0 / (1.0 + jnp.exp(-cn)))


def _pconv_kernel(prev_ref, cur_ref, w_ref, b_ref, lnw_ref, lnb_ref, o_ref, win_ref):
    i = pl.program_id(1)
    tc = cur_ref.shape[0]
    n = CONV_HALO + tc
    win_ref[0, 0:CONV_HALO, :] = jnp.where(i > 0, prev_ref[...], 0.0)
    win_ref[0, CONV_HALO:n, :] = cur_ref[...]
    win_ref[0, n:n + 8, :] = jnp.zeros((8, CONV_DIM), F32)
    step = n // 4
    for r in range(1, 8):
        for c0 in range(0, n, step):
            win_ref[r, c0:c0 + step, :] = win_ref[0, c0 + r:c0 + r + step, :]
    off = CONV_HALO - (CONV_W - 1)

    def chunk(c, carry):
        base = pl.multiple_of(c * CONV_RC, CONV_RC)
        acc = jnp.zeros((CONV_RC, CONV_DIM), F32) + b_ref[...]
        for k in range(CONV_W):
            shift = k + off
            rows = pl.ds(pl.multiple_of(base + (shift // 8) * 8, 8), CONV_RC)
            acc = acc + w_ref[k:k + 1, :] * win_ref[shift % 8, rows, :]
        o_ref[pl.ds(base, CONV_RC), :] = _ln_swish(acc, lnw_ref[...], lnb_ref[...]).astype(o_ref.dtype)
        return carry

    lax.fori_loop(0, tc // CONV_RC, chunk, 0)


def _pconv(u2d, batch, seq, cw, cb, lnw, lnb):
    tc = CONV_TC
    assert seq % tc == 0
    nt = seq // tc
    const = lambda shape: pl.BlockSpec(shape, lambda b, i: (0,) * len(shape))
    return pl.pallas_call(
        _pconv_kernel,
        grid=(batch, nt),
        in_specs=[
            pl.BlockSpec((CONV_HALO, CONV_DIM),
                         lambda b, i: (jnp.maximum((b * seq + i * tc) // CONV_HALO - 1, 0), 0)),
            pl.BlockSpec((tc, CONV_DIM), lambda b, i: (b * nt + i, 0)),
            const(cw.shape), const(cb.shape), const(lnw.shape), const(lnb.shape),
        ],
        out_specs=pl.BlockSpec((tc, CONV_DIM), lambda b, i: (b * nt + i, 0)),
        out_shape=jax.ShapeDtypeStruct((batch * seq, CONV_DIM), BF16),
        scratch_shapes=[pltpu.VMEM((8, CONV_HALO + tc + 8, CONV_DIM), F32)],
        compiler_params=_cparams(("arbitrary", "arbitrary")),
        name="pconv",
    )(u2d, u2d, cw, cb, lnw, lnb)


SCONV_R = 8


def _sconv_kernel(st_ref, u_ref, w_ref, b_ref, lnw_ref, lnb_ref, o_ref, ns_ref, win_ref):
    nprev = st_ref.shape[1]
    ds = u_ref.shape[1]
    for r in range(st_ref.shape[0]):
        win_ref[0:nprev, :] = st_ref[r]
        win_ref[nprev:nprev + ds, :] = u_ref[r]
        acc = jnp.zeros((ds, CONV_DIM), F32) + b_ref[...]
        for k in range(CONV_W):
            acc = acc + w_ref[k:k + 1, :] * win_ref[k:k + ds, :]
        o_ref[r] = _ln_swish(acc, lnw_ref[...], lnb_ref[...])
        ns_ref[r] = win_ref[ds:ds + nprev, :]


def _sconv(state, u3, cw, cb, lnw, lnb):
    db, nprev, _ = state.shape
    ds = u3.shape[1]
    r = SCONV_R
    assert db % r == 0 and nprev == CONV_W - 1
    const = lambda shape: pl.BlockSpec(shape, lambda i: (0,) * len(shape))
    return pl.pallas_call(
        _sconv_kernel,
        grid=(db // r,),
        in_specs=[
            pl.BlockSpec((r, nprev, CONV_DIM), lambda i: (i, 0, 0)),
            pl.BlockSpec((r, ds, CONV_DIM), lambda i: (i, 0, 0)),
            const(cw.shape), const(cb.shape), const(lnw.shape), const(lnb.shape),
        ],
        out_specs=[
            pl.BlockSpec((r, ds, CONV_DIM), lambda i: (i, 0, 0)),
            pl.BlockSpec((r, nprev, CONV_DIM), lambda i: (i, 0, 0)),
        ],
        out_shape=[
            jax.ShapeDtypeStruct((db, ds, CONV_DIM), F32),
            jax.ShapeDtypeStruct((db, nprev, CONV_DIM), F32),
        ],
        scratch_shapes=[pltpu.VMEM((nprev + ds + 2, CONV_DIM), F32)],
        compiler_params=_cparams(("arbitrary",)),
        name="sconv",
    )(state, u3, cw, cb, lnw, lnb)


PATTN_TQ = KV_TILE


def _pattn_kernel(lam_ref, qt_ref, kb_ref, vt_ref, bias_ref, sw_ref, o_ref, m_ref, l_ref, acc_ref, s_ref,
                  *, lam_init):
    i = pl.program_id(2)
    tk = KV_TILE
    m_ref[...] = jnp.full(m_ref.shape, -jnp.inf, F32)
    l_ref[...] = jnp.zeros(l_ref.shape, F32)
    acc_ref[...] = jnp.zeros(acc_ref.shape, F32)

    def scores(c, buf):
        k0 = pl.multiple_of(c * tk, tk)
        for mm in range(2):
            kt = kb_ref[pl.ds(k0, tk), mm * HEAD_DIM:(mm + 1) * HEAD_DIM]
            s_ref[buf, mm] = jnp.dot(kt, qt_ref[mm * HEAD_DIM:(mm + 1) * HEAD_DIM, :],
                                     preferred_element_type=F32)

    def consume(c, buf, bias):
        ps, alphas = [], []
        for mm in range(2):
            s = s_ref[buf, mm]
            if bias is not None:
                s = s + bias
            m_prev = m_ref[mm]
            m_new = jnp.maximum(m_prev, jnp.max(s, axis=0, keepdims=True))
            alpha = jnp.exp2(m_prev - m_new)
            p = jnp.exp2(s - m_new)
            l_ref[mm] = alpha * l_ref[mm] + jnp.sum(p, axis=0, keepdims=True)
            m_ref[mm] = m_new
            ps.append(p.astype(BF16))
            alphas.append(alpha)
        for mm in range(2):
            pv = jnp.dot(vt_ref[c], ps[mm], preferred_element_type=F32)
            acc_ref[mm] = alphas[mm] * acc_ref[mm] + pv

    scores(0, 0)

    def far_pair(t, carry):
        scores(2 * t + 1, 1)
        consume(2 * t, 0, None)
        scores(2 * t + 2, 0)
        consume(2 * t + 1, 1, None)
        return carry

    def far_quad(t, carry):
        far_pair(2 * t, carry)
        far_pair(2 * t + 1, carry)
        return carry

    n_far = jnp.maximum(i - 1, 0)
    n_quad = n_far // 4
    lax.fori_loop(0, n_quad, far_quad, 0)
    lax.fori_loop(2 * n_quad, n_far // 2, far_pair, 0)

    @pl.when(i == 0)
    def _():
        consume(0, 0, bias_ref[1])

    @pl.when(i % 2 == 1)
    def _():
        scores(i, 1)
        consume(i - 1, 0, bias_ref[0])
        consume(i, 1, bias_ref[1])

    @pl.when((i % 2 == 0) & (i > 0))
    def _():
        scores(i - 1, 1)
        consume(i - 2, 0, None)
        scores(i, 0)
        consume(i - 1, 1, bias_ref[0])
        consume(i, 0, bias_ref[1])

    lam = lam_ref[0]
    o = acc_ref[0] / l_ref[0] - lam * (acc_ref[1] / l_ref[1])
    ms = jnp.mean(o * o, axis=0, keepdims=True)
    on = ((o * lax.rsqrt(ms + EPS)) * sw_ref[...]) * (1.0 - lam_init)
    o_ref[...] = on.T.astype(o_ref.dtype)


def _pattn(lam, qt, kb, vt, bias_tiles, swc, batch, seq, lam_init):
    tq = PATTN_TQ
    tk = KV_TILE
    assert seq % tq == 0 and tq == tk
    nq = seq // tq
    nkv = seq // tk
    return pl.pallas_call(
        functools.partial(_pattn_kernel, lam_init=lam_init),
        grid=(batch, N_HEADS, nq),
        in_specs=[
            pl.BlockSpec(memory_space=pltpu.SMEM),
            pl.BlockSpec((V_DIM, tq), lambda b, h, i: (h, b * nq + i)),
            pl.BlockSpec((seq, V_DIM), lambda b, h, i: (b, h)),
            pl.BlockSpec((nkv, V_DIM, tk), lambda b, h, i: (b, h, 0)),
            pl.BlockSpec((None, 1 + tq // tk, tk, tq), lambda b, h, i: (h, 0, 0, 0)),
            pl.BlockSpec(swc.shape, lambda b, h, i: (0, 0)),
        ],
        out_specs=pl.BlockSpec((tq, V_DIM), lambda b, h, i: (b * nq + i, h)),
        out_shape=jax.ShapeDtypeStruct((batch * seq, ATTN_WIDTH), BF16),
        scratch_shapes=[
            pltpu.VMEM((2, 1, tq), F32),
            pltpu.VMEM((2, 1, tq), F32),
            pltpu.VMEM((2, V_DIM, tq), F32),
            pltpu.VMEM((2, 2, tk, tq), F32),
        ],
        compiler_params=_cparams(("arbitrary", "arbitrary", "arbitrary")),
        name="pattn",
    )(lam, qt, kb, vt, bias_tiles, swc)


SATTN_CP = 8
SATTN_NBUF = 4
ROWS_PER_PAGE = PAGE_SIZE * N_HEADS


def _sattn_kernel(pt_ref, lam_ref, q_ref, kn_ref, vn_ref, hm_ref, blast_ref, bnew_ref, sw_ref,
                  ck_ref, cv_ref, o_ref, kbuf, vbuf, ksem, vsem, *, n_req, n_chunks, lam_init):
    r = pl.program_id(0)
    cp = SATTN_CP
    nbuf = SATTN_NBUF
    rows = cp * ROWS_PER_PAGE

    def page_copies(req, chunk, slot):
        out = []
        for p in range(cp):
            pg = pt_ref[req, chunk * cp + p]
            out.append(pltpu.make_async_copy(ck_ref.at[0, pg], kbuf.at[slot, p], ksem.at[slot]))
            out.append(pltpu.make_async_copy(cv_ref.at[0, pg], vbuf.at[slot, p], vsem.at[slot]))
        return out

    def start_chunk(g):
        req = g // n_chunks
        chunk = g % n_chunks
        for c in page_copies(req, chunk, g % nbuf):
            c.start()

    @pl.when(r == 0)
    def _():
        for g in range(nbuf - 1):
            start_chunk(g)

    qs = q_ref[0]
    ds = qs.shape[0]
    lane = lax.broadcasted_iota(jnp.int32, (ds, V_DIM), 1)
    pieces = []
    for h in range(N_HEADS):
        qh = qs[:, h * V_DIM:(h + 1) * V_DIM]
        for mm in range(2):
            keep = (lane < HEAD_DIM) if mm == 0 else (lane >= HEAD_DIM)
            pieces.append(jnp.where(keep, qh, 0.0))
    q2 = jnp.concatenate(pieces, axis=0).astype(BF16)
    nrow = q2.shape[0]

    def step(s, state, v2):
        m_prev, l_prev, acc = state
        m_new = jnp.maximum(m_prev, jnp.max(s, axis=1, keepdims=True))
        alpha = jnp.exp2(m_prev - m_new)
        p = jnp.exp2(s - m_new)
        l_new = alpha * l_prev + jnp.sum(p, axis=1, keepdims=True)
        acc_new = alpha * acc + jnp.dot(p.astype(BF16), v2, preferred_element_type=F32)
        return m_new, l_new, acc_new

    nt = (((1,), (1,)), ((), ()))
    hm = hm_ref[...]
    hm_chunk = jnp.concatenate([hm] * cp, axis=1)
    last_mask = jnp.concatenate([hm] * (cp - 1) + [blast_ref[...]], axis=1)

    def chunk_body(c, state):
        g = r * n_chunks + c
        nxt = g + (nbuf - 1)

        @pl.when(nxt < n_req * n_chunks)
        def _():
            start_chunk(nxt)

        slot = g % nbuf
        for cpy in page_copies(r, c, slot):
            cpy.wait()
        k2 = kbuf[slot].reshape(rows, V_DIM).astype(BF16)
        v2 = vbuf[slot].reshape(rows, V_DIM).astype(BF16)
        s = lax.dot_general(q2, k2, nt, preferred_element_type=F32)
        s = s + jnp.where(c == n_chunks - 1, last_mask, hm_chunk)
        return step(s, state, v2)

    state = (jnp.full((nrow, 1), -jnp.inf, F32), jnp.zeros((nrow, 1), F32), jnp.zeros((nrow, V_DIM), F32))
    state = lax.fori_loop(0, n_chunks, chunk_body, state)

    kn = kn_ref[0].astype(BF16)
    vn = vn_ref[0].astype(BF16)
    s = lax.dot_general(q2, kn, nt, preferred_element_type=F32) + bnew_ref[...]
    _, l_fin, acc = step(s, state, vn)

    lam = lam_ref[0]
    o = acc / l_fin
    outs = []
    for h in range(N_HEADS):
        o1 = o[(2 * h) * ds:(2 * h + 1) * ds]
        o2 = o[(2 * h + 1) * ds:(2 * h + 2) * ds]
        oh = o1 - lam * o2
        ms = jnp.mean(oh * oh, axis=-1, keepdims=True)
        outs.append(((oh * lax.rsqrt(ms + EPS)) * sw_ref[...]) * (1.0 - lam_init))
    o_ref[0] = jnp.concatenate(outs, axis=1)


def _sattn(page_table, lam, q3, kn3, vn3, hm, blast, bnew, swr, cache_k, cache_v, lam_init):
    n_req, ds, _ = q3.shape
    n_pages = page_table.shape[1]
    assert n_pages % SATTN_CP == 0
    n_chunks = n_pages // SATTN_CP
    kernel = functools.partial(_sattn_kernel, n_req=n_req, n_chunks=n_chunks, lam_init=lam_init)
    const = lambda shape: pl.BlockSpec(shape, lambda i, pt: (0,) * len(shape))
    return pl.pallas_call(
        kernel,
        grid_spec=pltpu.PrefetchScalarGridSpec(
            num_scalar_prefetch=1,
            grid=(n_req,),
            in_specs=[
                pl.BlockSpec(memory_space=pltpu.SMEM),
                pl.BlockSpec((1, ds, ATTN_WIDTH), lambda i, pt: (i, 0, 0)),
                pl.BlockSpec((1, ds * N_HEADS, V_DIM), lambda i, pt: (i, 0, 0)),
                pl.BlockSpec((1, ds * N_HEADS, V_DIM), lambda i, pt: (i, 0, 0)),
                const(hm.shape), const(blast.shape), const(bnew.shape), const(swr.shape),
                pl.BlockSpec(memory_space=pl.ANY),
                pl.BlockSpec(memory_space=pl.ANY),
            ],
            out_specs=pl.BlockSpec((1, ds, ATTN_WIDTH), lambda i, pt: (i, 0, 0)),
            scratch_shapes=[
                pltpu.VMEM((SATTN_NBUF, SATTN_CP, PAGE_SIZE, N_HEADS, V_DIM), F32),
                pltpu.VMEM((SATTN_NBUF, SATTN_CP, PAGE_SIZE, N_HEADS, V_DIM), F32),
                pltpu.SemaphoreType.DMA((SATTN_NBUF,)),
                pltpu.SemaphoreType.DMA((SATTN_NBUF,)),
            ],
        ),
        out_shape=jax.ShapeDtypeStruct((n_req, ds, ATTN_WIDTH), F32),
        compiler_params=_cparams(("arbitrary",)),
        name="sattn",
    )(page_table, lam, q3, kn3, vn3, hm, blast, bnew, swr, cache_k, cache_v)


MIX_TM = 512
ROUTE_ROWS = 8
LANES = 128
SUBLANES = 8


def _store_row_tiles(ref, x):
    n, d = x.shape
    c = d // LANES
    for j in range(c):
        ref[pl.ds(j, n, stride=c), :] = x[:, j * LANES:(j + 1) * LANES]


def _load_row_tiles(ref, n, c):
    return jnp.concatenate([ref[pl.ds(j, n, stride=c), :] for j in range(c)], axis=1)


def _mix_kernel(a_ref, c_ref, x_ref, wo_ref, n2_ref, wr_ref, br_ref, tri_ref, cin_ref,
                x2_ref, h_ref, route_ref, cnt_ref):
    i = pl.program_id(0)
    tm = x_ref.shape[0]
    half = a_ref.shape[1]

    @pl.when(i == 0)
    def _():
        cnt_ref[...] = cin_ref[...]

    x2 = (x_ref[...]
          + jnp.dot(a_ref[...].astype(BF16), wo_ref[0:half, :], preferred_element_type=F32)
          + jnp.dot(c_ref[...].astype(BF16), wo_ref[half:, :], preferred_element_type=F32))
    x2_ref[...] = x2
    ms = jnp.mean(x2 * x2, axis=-1, keepdims=True)
    h = (x2 * lax.rsqrt(ms + EPS)) * n2_ref[...]
    _store_row_tiles(h_ref, h)

    h_hi = h.astype(BF16)
    h_lo = (h - h_hi.astype(F32)).astype(BF16)
    wr = wr_ref[...]
    w_hi = wr.astype(BF16)
    w_lo = (wr - w_hi.astype(F32)).astype(BF16)
    nt = (((1,), (1,)), ((), ()))
    logits = (lax.dot_general(w_hi, h_hi, nt, preferred_element_type=F32)
              + lax.dot_general(w_hi, h_lo, nt, preferred_element_type=F32)
              + lax.dot_general(w_lo, h_hi, nt, preferred_element_type=F32)) + br_ref[...]
    gl = logits[0:N_GROUPS]
    el = logits[N_GROUPS:N_GROUPS + N_EXPERTS]

    sub8 = lax.broadcasted_iota(jnp.int32, (N_GROUPS, tm), 0)

    def first_argmax(vals):
        top = jnp.max(vals, axis=0, keepdims=True)
        idx = jnp.min(jnp.where(vals == top, sub8, N_GROUPS), axis=0, keepdims=True)
        return top, idx

    gexp = jnp.exp(gl - jnp.max(gl, axis=0, keepdims=True))
    gp = gexp / jnp.sum(gexp, axis=0, keepdims=True)
    g_prob, g_idx = first_argmax(gp)

    el3 = el.reshape(N_GROUPS, EXPERTS_PER_GROUP, tm)
    sel = jnp.zeros((EXPERTS_PER_GROUP, tm), F32)
    for g in range(N_GROUPS):
        sel = sel + jnp.where(g_idx == g, el3[g], 0.0)
    v1, i1 = first_argmax(sel)
    v2, i2 = first_argmax(jnp.where(sub8 == i1, -jnp.inf, sel))
    e2 = jnp.exp(v2 - v1)
    p1 = 1.0 / (1.0 + e2)
    p2 = e2 / (1.0 + e2)
    ea = g_idx * EXPERTS_PER_GROUP + i1
    eb = g_idx * EXPERTS_PER_GROUP + i2

    sub64 = lax.broadcasted_iota(jnp.int32, (N_EXPERTS, tm), 0)
    oh_a = sub64 == ea
    oh_b = sub64 == eb
    oh = jnp.where(oh_a | oh_b, 1.0, 0.0)
    before = jnp.dot(oh.astype(BF16), tri_ref[...], preferred_element_type=F32) + cnt_ref[...]
    rank_a = jnp.sum(jnp.where(oh_a, before, 0.0), axis=0, keepdims=True)
    rank_b = jnp.sum(jnp.where(oh_b, before, 0.0), axis=0, keepdims=True)
    cnt_ref[...] = cnt_ref[...] + jnp.sum(oh, axis=1, keepdims=True)

    route_ref[...] = jnp.concatenate(
        [ea.astype(F32), eb.astype(F32), rank_a, rank_b, g_prob * p1, g_prob * p2,
         jnp.zeros((ROUTE_ROWS - 6, tm), F32)], axis=0)


def _mix(a, c, x2d, wo_b, n2, wr, br, tri, cnt_in):
    t, d = x2d.shape
    tm = MIX_TM
    assert t % tm == 0
    const = lambda shape: pl.BlockSpec(shape, lambda i: (0,) * len(shape))
    return pl.pallas_call(
        _mix_kernel,
        grid=(t // tm,),
        in_specs=[
            pl.BlockSpec((tm, a.shape[1]), lambda i: (i, 0)),
            pl.BlockSpec((tm, c.shape[1]), lambda i: (i, 0)),
            pl.BlockSpec((tm, d), lambda i: (i, 0)),
            const(wo_b.shape), const(n2.shape), const(wr.shape), const(br.shape), const(tri.shape),
            const(cnt_in.shape),
        ],
        out_specs=[
            pl.BlockSpec((tm, d), lambda i: (i, 0)),
            pl.BlockSpec((tm * (d // LANES), LANES), lambda i: (i, 0)),
            pl.BlockSpec((ROUTE_ROWS, tm), lambda i: (0, i)),
            const(cnt_in.shape),
        ],
        out_shape=[
            jax.ShapeDtypeStruct((t, d), F32),
            jax.ShapeDtypeStruct((t * (d // LANES), LANES), F32),
            jax.ShapeDtypeStruct((ROUTE_ROWS, t), F32),
            jax.ShapeDtypeStruct(cnt_in.shape, F32),
        ],
        compiler_params=_cparams(("arbitrary",)),
        name="mix",
    )(a, c, x2d, wo_b, n2, wr, br, tri, cnt_in)


ROW_TM = 512
ROW_UNROLL = 8
EXPERT_BM = 256


def _row_tile(row):
    return pl.ds(pl.multiple_of(row * SUBLANES, SUBLANES), SUBLANES)


def _dispatch_kernel(slots_ref, h_ref, xin_ref, xb_ref, sem):
    del xin_ref
    tm = slots_ref.shape[2]

    def issue(t, carry):
        for j in range(2):
            s = slots_ref[0, j, t]
            pltpu.make_async_copy(h_ref.at[_row_tile(t)], xb_ref.at[_row_tile(s)], sem).start(priority=j)
        return carry

    lax.fori_loop(0, tm, issue, 0, unroll=ROW_UNROLL)

    def drain(t, carry):
        for j in range(2):
            pltpu.make_async_copy(h_ref.at[_row_tile(0)], xb_ref.at[_row_tile(0)], sem).wait()
        return carry

    lax.fori_loop(0, tm, drain, 0, unroll=ROW_UNROLL)


def _dispatch(slots3, h, xb):
    tm = ROW_TM
    assert h.shape[0] % (tm * SUBLANES) == 0 and h.shape[1] == LANES
    return pl.pallas_call(
        _dispatch_kernel,
        grid=(h.shape[0] // (tm * SUBLANES),),
        in_specs=[
            pl.BlockSpec((1, 2, tm), lambda i: (i, 0, 0), memory_space=pltpu.SMEM),
            pl.BlockSpec((tm * SUBLANES, LANES), lambda i: (i, 0)),
            pl.BlockSpec(memory_space=pl.ANY),
        ],
        out_specs=pl.BlockSpec(memory_space=pl.ANY),
        out_shape=jax.ShapeDtypeStruct(xb.shape, xb.dtype),
        scratch_shapes=[pltpu.SemaphoreType.DMA],
        input_output_aliases={2: 0},
        compiler_params=_cparams(("arbitrary",)),
        name="dispatch",
    )(slots3, h, xb)


XB_SLOTS = 3


def _experts_kernel(be_ref, nb_ref, xb_hbm, w1_ref, w3_ref, w2_ref, yb_ref, w1b, w3b, w2b, xbuf, xsem):
    i = pl.program_id(0)
    nblk = pl.num_programs(0)
    bm = EXPERT_BM
    rows = bm * SUBLANES

    def xb_copy(blk, slot):
        src = xb_hbm.at[pl.ds(pl.multiple_of(blk * rows, rows), rows)]
        return pltpu.make_async_copy(src, xbuf.at[slot], xsem.at[slot])

    @pl.when(i == 0)
    def _():
        for blk in range(XB_SLOTS - 1):
            @pl.when(blk < nblk)
            def _():
                xb_copy(blk, blk).start()

    ahead = i + (XB_SLOTS - 1)

    @pl.when(ahead < nblk)
    def _():
        xb_copy(ahead, ahead % XB_SLOTS).start()

    slot = i % XB_SLOTS
    xb_copy(i, slot).wait()
    xb_ref = xbuf.at[slot]

    @pl.when((i == 0) | (be_ref[i] != be_ref[jnp.maximum(i - 1, 0)]))
    def _():
        w1b[...] = w1_ref[0].astype(BF16)
        w3b[...] = w3_ref[0].astype(BF16)
        w2b[...] = w2_ref[0].astype(BF16)

    @pl.when(i < nb_ref[0])
    def _():
        x = _load_row_tiles(xb_ref, bm, SUBLANES).astype(BF16)
        a = jnp.dot(x, w1b[...], preferred_element_type=F32)
        b = jnp.dot(x, w3b[...], preferred_element_type=F32)
        mid = (a * (1.0 / (1.0 + jnp.exp(-a)))) * b
        _store_row_tiles(yb_ref, jnp.dot(mid.astype(BF16), w2b[...], preferred_element_type=F32))

    @pl.when(i >= nb_ref[0])
    def _():
        yb_ref[...] = jnp.zeros(yb_ref.shape, F32)


def _experts(block_expert, n_used, xb, w1, w3, w2):
    bm = EXPERT_BM
    _, d, hid = w1.shape
    rows = bm * SUBLANES
    return pl.pallas_call(
        _experts_kernel,
        grid_spec=pltpu.PrefetchScalarGridSpec(
            num_scalar_prefetch=2,
            grid=(xb.shape[0] // rows,),
            in_specs=[
                pl.BlockSpec(memory_space=pl.ANY),
                pl.BlockSpec((1, d, hid), lambda i, be, nb: (be[i], 0, 0)),
                pl.BlockSpec((1, d, hid), lambda i, be, nb: (be[i], 0, 0)),
                pl.BlockSpec((1, hid, d), lambda i, be, nb: (be[i], 0, 0)),
            ],
            out_specs=pl.BlockSpec((rows, LANES), lambda i, be, nb: (i, 0)),
            scratch_shapes=[
                pltpu.VMEM((d, hid), BF16),
                pltpu.VMEM((d, hid), BF16),
                pltpu.VMEM((hid, d), BF16),
                pltpu.VMEM((XB_SLOTS, rows, LANES), F32),
                pltpu.SemaphoreType.DMA((XB_SLOTS,)),
            ],
        ),
        out_shape=jax.ShapeDtypeStruct(xb.shape, F32),
        compiler_params=_cparams(("arbitrary",)),
        name="experts",
    )(block_expert, n_used, xb, w1, w3, w2)


def _combine_kernel(slots_ref, x2_ref, g_ref, yb_ref, o_ref, ybuf, sem):
    tm = x2_ref.shape[0]

    def issue(t, carry):
        for j in range(2):
            s = slots_ref[0, j, t]
            pltpu.make_async_copy(yb_ref.at[_row_tile(s)], ybuf.at[j, _row_tile(t)], sem).start(priority=j)
        return carry

    lax.fori_loop(0, tm, issue, 0, unroll=ROW_UNROLL)

    def drain(t, carry):
        for j in range(2):
            pltpu.make_async_copy(yb_ref.at[_row_tile(0)], ybuf.at[j, _row_tile(0)], sem).wait()
        return carry

    lax.fori_loop(0, tm, drain, 0, unroll=ROW_UNROLL)
    g = g_ref[...]
    ya = _load_row_tiles(ybuf.at[0], tm, SUBLANES)
    yb = _load_row_tiles(ybuf.at[1], tm, SUBLANES)
    o_ref[...] = x2_ref[...] + (g[:, 0:1] * ya + g[:, 1:2] * yb)


def _combine(slots3, x2, gates, yb):
    t, d = x2.shape
    tm = ROW_TM
    assert t % tm == 0
    return pl.pallas_call(
        _combine_kernel,
        grid=(t // tm,),
        in_specs=[
            pl.BlockSpec((1, 2, tm), lambda i: (i, 0, 0), memory_space=pltpu.SMEM),
            pl.BlockSpec((tm, d), lambda i: (i, 0)),
            pl.BlockSpec((tm, 2), lambda i: (i, 0)),
            pl.BlockSpec(memory_space=pl.ANY),
        ],
        out_specs=pl.BlockSpec((tm, d), lambda i: (i, 0)),
        out_shape=jax.ShapeDtypeStruct((t, d), F32),
        scratch_shapes=[pltpu.VMEM((2, tm * SUBLANES, LANES), F32), pltpu.SemaphoreType.DMA],
        compiler_params=_cparams(("arbitrary",)),
        name="combine",
    )(slots3, x2, gates, yb)


def _t5_bucket(dist):
    n = np.maximum(dist, 0)
    max_exact = N_BUCKETS // 2
    nf = np.maximum(n, 1).astype(np.float32)
    large = max_exact + (np.log(nf / np.float32(max_exact)) / np.float32(math.log(MAX_DISTANCE / max_exact))
                         * np.float32(N_BUCKETS - max_exact)).astype(np.int32)
    large = np.minimum(large, N_BUCKETS - 1)
    return np.where(n < max_exact, n, large)


def _bias_table(rel_bias, d_lo, d_hi):
    dist = np.arange(d_lo, d_hi)
    onehot = (_t5_bucket(dist)[None, :] == np.arange(N_BUCKETS)[:, None]).astype(np.float32)
    tab = jnp.sum(rel_bias.T[:, :, None] * onehot[None], axis=1)
    tab = (tab - rel_bias[N_BUCKETS - 1][:, None]) * LOG2E
    return jnp.where(jnp.asarray(dist >= 0)[None, :], tab, MASKED)


def _toeplitz(v, nr, nc):
    n = nr + nc - 1
    assert v.shape[1] == n
    w = jnp.concatenate([v, jnp.zeros((v.shape[0], 1), v.dtype)], axis=1)
    flat = jnp.tile(w, (1, nr))[:, :nr * n]
    return flat.reshape(v.shape[0], nr, n)[:, :, nr - 1:nr - 1 + nc]


def _layer(xp2, xs2, batch, seq, db, ds, cache_k, cache_v, state_conv, page_table, rel_bias, lam, lam_init, p):
    d = xp2.shape[1]
    w = ATTN_WIDTH
    qwc = (jnp.tile(p["q_norm_w"], 2 * N_HEADS) * (HEAD_DIM ** -0.5 * LOG2E)).reshape(w, 1)
    kw = jnp.tile(p["k_norm_w"], 2 * N_HEADS).reshape(1, w)
    gid = np.arange(w) // HEAD_DIM
    gmat = jnp.asarray((gid[:, None] == gid[None, :]).astype(np.float32) / HEAD_DIM, BF16)
    n1 = p["norm1_w"].reshape(1, d)
    win_b = p["w_in"].astype(BF16)
    lam1 = lam.reshape(1).astype(F32)

    qt_p, kb_p, k32_p, vt_p, v32_p, u_p = _proj(xp2, n1, win_b, qwc, kw, gmat)
    qt_s, _, k32_s, _, v32_s, u_s = _proj(xs2, n1, win_b, qwc, kw, gmat)

    cw = p["conv_w"]
    cb = p["conv_b"].reshape(1, CONV_DIM)
    lnw = p["conv_ln_w"].reshape(1, CONV_DIM)
    lnb = p["conv_ln_b"].reshape(1, CONV_DIM)
    oc_p = _pconv(u_p, batch, seq, cw, cb, lnw, lnb)
    oc_s, conv_s = _sconv(state_conv, u_s.reshape(db, ds, CONV_DIM), cw, cb, lnw, lnb)
    conv_p = u_p.reshape(batch, seq, CONV_DIM)[:, seq - (CONV_W - 1):]

    tq, tk = PATTN_TQ, KV_TILE
    bias_tiles = jnp.stack(
        [_toeplitz(_bias_table(rel_bias, (1 - c) * tk - (tk - 1), (1 - c) * tk + tq), tk, tq)
         for c in range(1 + tq // tk)], axis=1)
    swc = p["subln_w"].reshape(V_DIM, 1)
    a_p = _pattn(lam1, qt_p, kb_p, vt_p, bias_tiles, swc, batch, seq, lam_init)

    nrow = 2 * N_HEADS * ds
    eye = jnp.asarray(np.eye(N_HEADS, dtype=bool))

    def per_head_rows(t3):
        full = jnp.where(eye[:, None, None, None, :], t3[:, None, :, :, None], MASKED)
        full = jnp.broadcast_to(full, (N_HEADS, 2, ds, t3.shape[2], N_HEADS))
        return full.reshape(nrow, t3.shape[2] * N_HEADS)

    hm = per_head_rows(jnp.zeros((N_HEADS, ds, PAGE_SIZE), F32))
    blast = per_head_rows(_toeplitz(_bias_table(rel_bias, 1, PAGE_SIZE + ds)[:, ::-1], ds, PAGE_SIZE))
    bnew = per_head_rows(_toeplitz(_bias_table(rel_bias, -(ds - 1), ds)[:, ::-1], ds, ds))
    swr = p["subln_w"].reshape(1, V_DIM)
    q_s = (qt_s.T.astype(F32)).reshape(db, ds, w)
    kn3 = k32_s.reshape(db, ds * N_HEADS, V_DIM)
    vn3 = v32_s.reshape(db, ds * N_HEADS, V_DIM)
    a_s = _sattn(page_table, lam1, q_s, kn3, vn3, hm, blast, bnew, swr, cache_k, cache_v, lam_init)

    wo_b = p["w_out"].astype(BF16)
    n2 = p["norm2_w"].reshape(1, d)
    wr = jnp.concatenate([p["w_group"], p["w_expert"]], axis=1).T
    br = jnp.concatenate([p["b_group"], p["b_expert"]]).reshape(-1, 1)
    tri = jnp.asarray(np.triu(np.ones((MIX_TM, MIX_TM), np.float32), 1), BF16)
    cnt0 = jnp.zeros((N_EXPERTS, 1), F32)
    x2_p, h_p, route_p, cnt_p = _mix(a_p, oc_p, xp2, wo_b, n2, wr, br, tri, cnt0)
    x2_s, h_s, route_s, cnt = _mix(a_s.reshape(db * ds, w), oc_s.reshape(db * ds, CONV_DIM), xs2,
                                   wo_b, n2, wr, br, tri, cnt_p)

    bm = EXPERT_BM
    counts = cnt[:, 0].astype(jnp.int32)
    padded = (counts + bm - 1) // bm * bm
    pad_end = jnp.cumsum(padded)
    pad_start = pad_end - padded
    n_assign = 2 * (xp2.shape[0] + xs2.shape[0])
    n_blocks = -(-n_assign // bm) + N_EXPERTS
    nslot = n_blocks * bm
    block_first = jnp.arange(n_blocks, dtype=jnp.int32) * bm
    block_expert = jnp.minimum(
        jnp.sum((pad_end[None, :] <= block_first[:, None]).astype(jnp.int32), axis=1), N_EXPERTS - 1)
    n_used = (pad_end[-1] // bm).reshape(1).astype(jnp.int32)
    expert_ids = jnp.arange(N_EXPERTS, dtype=jnp.int32)[:, None, None]

    def slots_of(route):
        e = route[0:2].astype(jnp.int32)
        rank = route[2:4].astype(jnp.int32)
        s = jnp.sum(jnp.where(e[None] == expert_ids, pad_start[:, None, None], 0), axis=0) + rank
        t = s.shape[1]
        return s.reshape(2, t // ROW_TM, ROW_TM).transpose(1, 0, 2)

    slots_p, slots_s = slots_of(route_p), slots_of(route_s)

    assert d == SUBLANES * LANES, "a model row must be exactly one 8x128 tile for the row-tile DMAs"
    xb = jnp.zeros((nslot * SUBLANES, LANES), F32)
    xb = _dispatch(slots_p, h_p, xb)
    xb = _dispatch(slots_s, h_s, xb)
    yb = _experts(block_expert, n_used, xb, p["w1"], p["w3"], p["w2"])
    y_p = _combine(slots_p, x2_p, route_p[4:6].T, yb)
    y_s = _combine(slots_s, x2_s, route_s[4:6].T, yb)

    kshape = (N_HEADS, V_DIM)
    return (y_p, y_s,
            k32_p.reshape(batch, seq, *kshape), v32_p.reshape(batch, seq, *kshape), conv_p,
            k32_s.reshape(db, ds, *kshape), v32_s.reshape(db, ds, *kshape), conv_s)


def kernel(x_prompt, x_sample, cache_k, cache_v, state_conv, page_table, rel_bias, norm1_w, w_in, q_norm_w, k_norm_w, lambda_q1, lambda_k1, lambda_q2, lambda_k2, subln_w, conv_w, conv_b, conv_ln_w, conv_ln_b, w_out, norm2_w, w_group, b_group, w_expert, b_expert, w1, w3, w2):
    batch, seq, d = x_prompt.shape
    db, ds, _ = x_sample.shape
    depth = w_in.shape[0]
    assert depth == 1, "paged caches are read once; a deeper trunk needs one cache slice per layer"
    xp = x_prompt.reshape(batch * seq, d)
    xs = x_sample.reshape(db * ds, d)
    l = 0
    lam_init = 0.8 - 0.6 * math.exp(-0.3 * l)
    lam = (jnp.exp(jnp.sum(lambda_q1[l].astype(F32) * lambda_k1[l].astype(F32)))
           - jnp.exp(jnp.sum(lambda_q2[l].astype(F32) * lambda_k2[l].astype(F32)))
           + lam_init)
    params = dict(norm1_w=norm1_w[l], w_in=w_in[l], q_norm_w=q_norm_w[l], k_norm_w=k_norm_w[l],
                  subln_w=subln_w[l], conv_w=conv_w[l], conv_b=conv_b[l], conv_ln_w=conv_ln_w[l],
                  conv_ln_b=conv_ln_b[l], w_out=w_out[l], norm2_w=norm2_w[l], w_group=w_group[l],
                  b_group=b_group[l], w_expert=w_expert[l], b_expert=b_expert[l],
                  w1=w1[l], w3=w3[l], w2=w2[l])
    y_p, y_s, k_p, v_p, c_p, k_s, v_s, c_s = _layer(
        xp, xs, batch, seq, db, ds, cache_k, cache_v, state_conv[l], page_table, rel_bias, lam, lam_init, params)
    return (y_p.reshape(batch, seq, d), y_s.reshape(db, ds, d),
            k_p[None], v_p[None], c_p[None], k_s[None], v_s[None], c_s[None])
```

```python
import functools
import math

import numpy as np
import jax
import jax.numpy as jnp
from jax import lax
from jax.experimental import pallas as pl
from jax.experimental.pallas import tpu as pltpu

F32 = jnp.float32
BF16 = jnp.bfloat16

N_HEADS = 4
HEAD_DIM = 64
V_DIM = 2 * HEAD_DIM
ATTN_WIDTH = N_HEADS * V_DIM
CONV_DIM = 512
CONV_W = 31
N_BUCKETS = 32
MAX_DISTANCE = 128
N_GROUPS = 8
EXPERTS_PER_GROUP = 8
N_EXPERTS = N_GROUPS * EXPERTS_PER_GROUP
PAGE_SIZE = 128
EPS = 1e-6
MASKED = -1e30

VMEM_LIMIT = 56 * 1024 * 1024


def _cparams(sem):
    return pltpu.CompilerParams(dimension_semantics=sem, vmem_limit_bytes=VMEM_LIMIT)


PROJ_TM = 512
KV_TILE = 512
LOG2E = math.log2(math.e)


def _proj_kernel(x_ref, n1_ref, win_ref, qwc_ref, kw_ref, g_ref,
                 qt_ref, kb_ref, k32_ref, vt_ref, v32_ref, u_ref):
    tm = x_ref.shape[0]
    x = x_ref[...]
    ms = jnp.mean(x * x, axis=-1, keepdims=True)
    xn = (x * lax.rsqrt(ms + EPS)) * n1_ref[...]
    z = jnp.dot(xn.astype(BF16), win_ref[...], preferred_element_type=F32)
    w = ATTN_WIDTH
    zq, zk, zv = z[:, :w], z[:, w:2 * w], z[:, 2 * w:3 * w]
    g1, g2 = z[:, 3 * w:3 * w + CONV_DIM], z[:, 3 * w + CONV_DIM:]

    kk = zk * zk
    hi = kk.astype(BF16)
    lo = (kk - hi.astype(F32)).astype(BF16)
    kms = (jnp.dot(hi, g_ref[...], preferred_element_type=F32)
           + jnp.dot(lo, g_ref[...], preferred_element_type=F32))
    kn = (zk * lax.rsqrt(kms + EPS)) * kw_ref[...]
    kb_ref[...] = kn.astype(BF16)
    for h in range(N_HEADS):
        k32_ref[pl.ds(h, tm, stride=N_HEADS), :] = kn[:, h * V_DIM:(h + 1) * V_DIM]
        v32_ref[pl.ds(h, tm, stride=N_HEADS), :] = zv[:, h * V_DIM:(h + 1) * V_DIM]
    zvt = zv.T
    for t in range(tm // KV_TILE):
        vt_ref[t] = zvt[:, t * KV_TILE:(t + 1) * KV_TILE].astype(BF16)

    q3 = zq.T.reshape(2 * N_HEADS, HEAD_DIM, tm)
    qms = jnp.mean(q3 * q3, axis=1, keepdims=True)
    qn = (q3 * lax.rsqrt(qms + EPS)).reshape(w, tm) * qwc_ref[...]
    qt_ref[...] = qn.astype(BF16)

    u_ref[...] = g1 * (1.0 / (1.0 + jnp.exp(-g2)))


def _proj(x2d, n1, win_b, qwc, kw, gmat):
    t = x2d.shape[0]
    tm = PROJ_TM
    assert t % tm == 0
    w = ATTN_WIDTH
    const = lambda shape: pl.BlockSpec(shape, lambda i: (0,) * len(shape))
    return pl.pallas_call(
        _proj_kernel,
        grid=(t // tm,),
        in_specs=[
            pl.BlockSpec((tm, x2d.shape[1]), lambda i: (i, 0)),
            const(n1.shape), const(win_b.shape), const(qwc.shape), const(kw.shape), const(gmat.shape),
        ],
        out_specs=[
            pl.BlockSpec((w, tm), lambda i: (0, i)),
            pl.BlockSpec((tm, w), lambda i: (i, 0)),
            pl.BlockSpec((tm * N_HEADS, V_DIM), lambda i: (i, 0)),
            pl.BlockSpec((tm // KV_TILE, w, KV_TILE), lambda i: (i, 0, 0)),
            pl.BlockSpec((tm * N_HEADS, V_DIM), lambda i: (i, 0)),
            pl.BlockSpec((tm, CONV_DIM), lambda i: (i, 0)),
        ],
        out_shape=[
            jax.ShapeDtypeStruct((w, t), BF16),
            jax.ShapeDtypeStruct((t, w), BF16),
            jax.ShapeDtypeStruct((t * N_HEADS, V_DIM), F32),
            jax.ShapeDtypeStruct((t // KV_TILE, w, KV_TILE), BF16),
            jax.ShapeDtypeStruct((t * N_HEADS, V_DIM), F32),
            jax.ShapeDtypeStruct((t, CONV_DIM), F32),
        ],
        compiler_params=_cparams(("arbitrary",)),
        name="proj",
    )(x2d, n1, win_b, qwc, kw, gmat)


CONV_TC = 512
CONV_RC = 64
CONV_HALO = 32


def _ln_swish(c, lnw, lnb):
    mu = jnp.mean(c, axis=-1, keepdims=True)
    d = c - mu
    var = jnp.mean(d * d, axis=-1, keepdims=True)
    cn = (d * lax.rsqrt(var + EPS)) * lnw + lnb
    return cn * (1.0 / (1.0 + jnp.exp(-cn)))


def _pconv_kernel(prev_ref, cur_ref, w_ref, b_ref, lnw_ref, lnb_ref, o_ref, win_ref):
    i = pl.program_id(1)
    tc = cur_ref.shape[0]
    n = CONV_HALO + tc
    win_ref[0, 0:CONV_HALO, :] = jnp.where(i > 0, prev_ref[...], 0.0)
    win_ref[0, CONV_HALO:n, :] = cur_ref[...]
    win_ref[0, n:n + 8, :] = jnp.zeros((8, CONV_DIM), F32)
    step = n // 4
    for r in range(1, 8):
        for c0 in range(0, n, step):
            win_ref[r, c0:c0 + step, :] = win_ref[0, c0 + r:c0 + r + step, :]
    off = CONV_HALO - (CONV_W - 1)

    def chunk(c, carry):
        base = pl.multiple_of(c * CONV_RC, CONV_RC)
        acc = jnp.zeros((CONV_RC, CONV_DIM), F32) + b_ref[...]
        for k in range(CONV_W):
            shift = k + off
            rows = pl.ds(pl.multiple_of(base + (shift // 8) * 8, 8), CONV_RC)
            acc = acc + w_ref[k:k + 1, :] * win_ref[shift % 8, rows, :]
        o_ref[pl.ds(base, CONV_RC), :] = _ln_swish(acc, lnw_ref[...], lnb_ref[...]).astype(o_ref.dtype)
        return carry

    lax.fori_loop(0, tc // CONV_RC, chunk, 0)


def _pconv(u2d, batch, seq, cw, cb, lnw, lnb):
    tc = CONV_TC
    assert seq % tc == 0
    nt = seq // tc
    const = lambda shape: pl.BlockSpec(shape, lambda b, i: (0,) * len(shape))
    return pl.pallas_call(
        _pconv_kernel,
        grid=(batch, nt),
        in_specs=[
            pl.BlockSpec((CONV_HALO, CONV_DIM),
                         lambda b, i: (jnp.maximum((b * seq + i * tc) // CONV_HALO - 1, 0), 0)),
            pl.BlockSpec((tc, CONV_DIM), lambda b, i: (b * nt + i, 0)),
            const(cw.shape), const(cb.shape), const(lnw.shape), const(lnb.shape),
        ],
        out_specs=pl.BlockSpec((tc, CONV_DIM), lambda b, i: (b * nt + i, 0)),
        out_shape=jax.ShapeDtypeStruct((batch * seq, CONV_DIM), BF16),
        scratch_shapes=[pltpu.VMEM((8, CONV_HALO + tc + 8, CONV_DIM), F32)],
        compiler_params=_cparams(("arbitrary", "arbitrary")),
        name="pconv",
    )(u2d, u2d, cw, cb, lnw, lnb)


SCONV_R = 8


def _sconv_kernel(st_ref, u_ref, w_ref, b_ref, lnw_ref, lnb_ref, o_ref, ns_ref, win_ref):
    nprev = st_ref.shape[1]
    ds = u_ref.shape[1]
    for r in range(st_ref.shape[0]):
        win_ref[0:nprev, :] = st_ref[r]
        win_ref[nprev:nprev + ds, :] = u_ref[r]
        acc = jnp.zeros((ds, CONV_DIM), F32) + b_ref[...]
        for k in range(CONV_W):
            acc = acc + w_ref[k:k + 1, :] * win_ref[k:k + ds, :]
        o_ref[r] = _ln_swish(acc, lnw_ref[...], lnb_ref[...])
        ns_ref[r] = win_ref[ds:ds + nprev, :]


def _sconv(state, u3, cw, cb, lnw, lnb):
    db, nprev, _ = state.shape
    ds = u3.shape[1]
    r = SCONV_R
    assert db % r == 0 and nprev == CONV_W - 1
    const = lambda shape: pl.BlockSpec(shape, lambda i: (0,) * len(shape))
    return pl.pallas_call(
        _sconv_kernel,
        grid=(db // r,),
        in_specs=[
            pl.BlockSpec((r, nprev, CONV_DIM), lambda i: (i, 0, 0)),
            pl.BlockSpec((r, ds, CONV_DIM), lambda i: (i, 0, 0)),
            const(cw.shape), const(cb.shape), const(lnw.shape), const(lnb.shape),
        ],
        out_specs=[
            pl.BlockSpec((r, ds, CONV_DIM), lambda i: (i, 0, 0)),
            pl.BlockSpec((r, nprev, CONV_DIM), lambda i: (i, 0, 0)),
        ],
        out_shape=[
            jax.ShapeDtypeStruct((db, ds, CONV_DIM), F32),
            jax.ShapeDtypeStruct((db, nprev, CONV_DIM), F32),
        ],
        scratch_shapes=[pltpu.VMEM((nprev + ds + 2, CONV_DIM), F32)],
        compiler_params=_cparams(("arbitrary",)),
        name="sconv",
    )(state, u3, cw, cb, lnw, lnb)


PATTN_TQ = KV_TILE


def _pattn_kernel(lam_ref, qt_ref, kb_ref, vt_ref, bias_ref, sw_ref, o_ref, m_ref, l_ref, acc_ref, s_ref,
                  *, lam_init):
    i = pl.program_id(2)
    tk = KV_TILE
    m_ref[...] = jnp.full(m_ref.shape, -jnp.inf, F32)
    l_ref[...] = jnp.zeros(l_ref.shape, F32)
    acc_ref[...] = jnp.zeros(acc_ref.shape, F32)

    def scores(c, buf):
        k0 = pl.multiple_of(c * tk, tk)
        for mm in range(2):
            kt = kb_ref[pl.ds(k0, tk), mm * HEAD_DIM:(mm + 1) * HEAD_DIM]
            s_ref[buf, mm] = jnp.dot(kt, qt_ref[mm * HEAD_DIM:(mm + 1) * HEAD_DIM, :],
                                     preferred_element_type=F32)

    def consume(c, buf, bias):
        ps, alphas = [], []
        for mm in range(2):
            s = s_ref[buf, mm]
            if bias is not None:
                s = s + bias
            m_prev = m_ref[mm]
            m_new = jnp.maximum(m_prev, jnp.max(s, axis=0, keepdims=True))
            alpha = jnp.exp2(m_prev - m_new)
            p = jnp.exp2(s - m_new)
            l_ref[mm] = alpha * l_ref[mm] + jnp.sum(p, axis=0, keepdims=True)
            m_ref[mm] = m_new
            ps.append(p.astype(BF16))
            alphas.append(alpha)
        for mm in range(2):
            pv = jnp.dot(vt_ref[c], ps[mm], preferred_element_type=F32)
            acc_ref[mm] = alphas[mm] * acc_ref[mm] + pv

    scores(0, 0)

    def far_pair(t, carry):
        scores(2 * t + 1, 1)
        consume(2 * t, 0, None)
        scores(2 * t + 2, 0)
        consume(2 * t + 1, 1, None)
        return carry

    def far_quad(t, carry):
        far_pair(2 * t, carry)
        far_pair(2 * t + 1, carry)
        return carry

    n_far = jnp.maximum(i - 1, 0)
    n_quad = n_far // 4
    lax.fori_loop(0, n_quad, far_quad, 0)
    lax.fori_loop(2 * n_quad, n_far // 2, far_pair, 0)

    @pl.when(i == 0)
    def _():
        consume(0, 0, bias_ref[1])

    @pl.when(i % 2 == 1)
    def _():
        scores(i, 1)
        consume(i - 1, 0, bias_ref[0])
        consume(i, 1, bias_ref[1])

    @pl.when((i % 2 == 0) & (i > 0))
    def _():
        scores(i - 1, 1)
        consume(i - 2, 0, None)
        scores(i, 0)
        consume(i - 1, 1, bias_ref[0])
        consume(i, 0, bias_ref[1])

    lam = lam_ref[0]
    o = acc_ref[0] / l_ref[0] - lam * (acc_ref[1] / l_ref[1])
    ms = jnp.mean(o * o, axis=0, keepdims=True)
    on = ((o * lax.rsqrt(ms + EPS)) * sw_ref[...]) * (1.0 - lam_init)
    o_ref[...] = on.T.astype(o_ref.dtype)


def _pattn(lam, qt, kb, vt, bias_tiles, swc, batch, seq, lam_init):
    tq = PATTN_TQ
    tk = KV_TILE
    assert seq % tq == 0 and tq == tk
    nq = seq // tq
    nkv = seq // tk
    return pl.pallas_call(
        functools.partial(_pattn_kernel, lam_init=lam_init),
        grid=(batch, N_HEADS, nq),
        in_specs=[
            pl.BlockSpec(memory_space=pltpu.SMEM),
            pl.BlockSpec((V_DIM, tq), lambda b, h, i: (h, b * nq + i)),
            pl.BlockSpec((seq, V_DIM), lambda b, h, i: (b, h)),
            pl.BlockSpec((nkv, V_DIM, tk), lambda b, h, i: (b, h, 0)),
            pl.BlockSpec((None, 1 + tq // tk, tk, tq), lambda b, h, i: (h, 0, 0, 0)),
            pl.BlockSpec(swc.shape, lambda b, h, i: (0, 0)),
        ],
        out_specs=pl.BlockSpec((tq, V_DIM), lambda b, h, i: (b * nq + i, h)),
        out_shape=jax.ShapeDtypeStruct((batch * seq, ATTN_WIDTH), BF16),
        scratch_shapes=[
            pltpu.VMEM((2, 1, tq), F32),
            pltpu.VMEM((2, 1, tq), F32),
            pltpu.VMEM((2, V_DIM, tq), F32),
            pltpu.VMEM((2, 2, tk, tq), F32),
        ],
        compiler_params=_cparams(("arbitrary", "arbitrary", "arbitrary")),
        name="pattn",
    )(lam, qt, kb, vt, bias_tiles, swc)


SATTN_CP = 8
SATTN_NBUF = 4
ROWS_PER_PAGE = PAGE_SIZE * N_HEADS


def _sattn_kernel(pt_ref, lam_ref, q_ref, kn_ref, vn_ref, hm_ref, blast_ref, bnew_ref, sw_ref,
                  ck_ref, cv_ref, o_ref, kbuf, vbuf, ksem, vsem, *, n_req, n_chunks, lam_init):
    r = pl.program_id(0)
    cp = SATTN_CP
    nbuf = SATTN_NBUF
    rows = cp * ROWS_PER_PAGE

    def page_copies(req, chunk, slot):
        out = []
        for p in range(cp):
            pg = pt_ref[req, chunk * cp + p]
            out.append(pltpu.make_async_copy(ck_ref.at[0, pg], kbuf.at[slot, p], ksem.at[slot]))
            out.append(pltpu.make_async_copy(cv_ref.at[0, pg], vbuf.at[slot, p], vsem.at[slot]))
        return out

    def start_chunk(g):
        req = g // n_chunks
        chunk = g % n_chunks
        for c in page_copies(req, chunk, g % nbuf):
            c.start()

    @pl.when(r == 0)
    def _():
        for g in range(nbuf - 1):
            start_chunk(g)

    qs = q_ref[0]
    ds = qs.shape[0]
    lane = lax.broadcasted_iota(jnp.int32, (ds, V_DIM), 1)
    pieces = []
    for h in range(N_HEADS):
        qh = qs[:, h * V_DIM:(h + 1) * V_DIM]
        for mm in range(2):
            keep = (lane < HEAD_DIM) if mm == 0 else (lane >= HEAD_DIM)
            pieces.append(jnp.where(keep, qh, 0.0))
    q2 = jnp.concatenate(pieces, axis=0).astype(BF16)
    nrow = q2.shape[0]

    def step(s, state, v2):
        m_prev, l_prev, acc = state
        m_new = jnp.maximum(m_prev, jnp.max(s, axis=1, keepdims=True))
        alpha = jnp.exp2(m_prev - m_new)
        p = jnp.exp2(s - m_new)
        l_new = alpha * l_prev + jnp.sum(p, axis=1, keepdims=True)
        acc_new = alpha * acc + jnp.dot(p.astype(BF16), v2, preferred_element_type=F32)
        return m_new, l_new, acc_new

    nt = (((1,), (1,)), ((), ()))
    hm = hm_ref[...]
    hm_chunk = jnp.concatenate([hm] * cp, axis=1)
    last_mask = jnp.concatenate([hm] * (cp - 1) + [blast_ref[...]], axis=1)

    def chunk_body(c, state):
        g = r * n_chunks + c
        nxt = g + (nbuf - 1)

        @pl.when(nxt < n_req * n_chunks)
        def _():
            start_chunk(nxt)

        slot = g % nbuf
        for cpy in page_copies(r, c, slot):
            cpy.wait()
        k2 = kbuf[slot].reshape(rows, V_DIM).astype(BF16)
        v2 = vbuf[slot].reshape(rows, V_DIM).astype(BF16)
        s = lax.dot_general(q2, k2, nt, preferred_element_type=F32)
        s = s + jnp.where(c == n_chunks - 1, last_mask, hm_chunk)
        return step(s, state, v2)

    state = (jnp.full((nrow, 1), -jnp.inf, F32), jnp.zeros((nrow, 1), F32), jnp.zeros((nrow, V_DIM), F32))
    state = lax.fori_loop(0, n_chunks, chunk_body, state)

    kn = kn_ref[0].astype(BF16)
    vn = vn_ref[0].astype(BF16)
    s = lax.dot_general(q2, kn, nt, preferred_element_type=F32) + bnew_ref[...]
    _, l_fin, acc = step(s, state, vn)

    lam = lam_ref[0]
    o = acc / l_fin
    outs = []
    for h in range(N_HEADS):
        o1 = o[(2 * h) * ds:(2 * h + 1) * ds]
        o2 = o[(2 * h + 1) * ds:(2 * h + 2) * ds]
        oh = o1 - lam * o2
        ms = jnp.mean(oh * oh, axis=-1, keepdims=True)
        outs.append(((oh * lax.rsqrt(ms + EPS)) * sw_ref[...]) * (1.0 - lam_init))
    o_ref[0] = jnp.concatenate(outs, axis=1)


def _sattn(page_table, lam, q3, kn3, vn3, hm, blast, bnew, swr, cache_k, cache_v, lam_init):
    n_req, ds, _ = q3.shape
    n_pages = page_table.shape[1]
    assert n_pages % SATTN_CP == 0
    n_chunks = n_pages // SATTN_CP
    kernel = functools.partial(_sattn_kernel, n_req=n_req, n_chunks=n_chunks, lam_init=lam_init)
    const = lambda shape: pl.BlockSpec(shape, lambda i, pt: (0,) * len(shape))
    return pl.pallas_call(
        kernel,
        grid_spec=pltpu.PrefetchScalarGridSpec(
            num_scalar_prefetch=1,
            grid=(n_req,),
            in_specs=[
                pl.BlockSpec(memory_space=pltpu.SMEM),
                pl.BlockSpec((1, ds, ATTN_WIDTH), lambda i, pt: (i, 0, 0)),
                pl.BlockSpec((1, ds * N_HEADS, V_DIM), lambda i, pt: (i, 0, 0)),
                pl.BlockSpec((1, ds * N_HEADS, V_DIM), lambda i, pt: (i, 0, 0)),
                const(hm.shape), const(blast.shape), const(bnew.shape), const(swr.shape),
                pl.BlockSpec(memory_space=pl.ANY),
                pl.BlockSpec(memory_space=pl.ANY),
            ],
            out_specs=pl.BlockSpec((1, ds, ATTN_WIDTH), lambda i, pt: (i, 0, 0)),
            scratch_shapes=[
                pltpu.VMEM((SATTN_NBUF, SATTN_CP, PAGE_SIZE, N_HEADS, V_DIM), F32),
                pltpu.VMEM((SATTN_NBUF, SATTN_CP, PAGE_SIZE, N_HEADS, V_DIM), F32),
                pltpu.SemaphoreType.DMA((SATTN_NBUF,)),
                pltpu.SemaphoreType.DMA((SATTN_NBUF,)),
            ],
        ),
        out_shape=jax.ShapeDtypeStruct((n_req, ds, ATTN_WIDTH), F32),
        compiler_params=_cparams(("arbitrary",)),
        name="sattn",
    )(page_table, lam, q3, kn3, vn3, hm, blast, bnew, swr, cache_k, cache_v)


MIX_TM = 512
ROUTE_ROWS = 8
LANES = 128
SUBLANES = 8


def _store_row_tiles(ref, x):
    n, d = x.shape
    c = d // LANES
    for j in range(c):
        ref[pl.ds(j, n, stride=c), :] = x[:, j * LANES:(j + 1) * LANES]


def _load_row_tiles(ref, n, c):
    return jnp.concatenate([ref[pl.ds(j, n, stride=c), :] for j in range(c)], axis=1)


def _mix_kernel(a_ref, c_ref, x_ref, wo_ref, n2_ref, wr_ref, br_ref, tri_ref, cin_ref,
                x2_ref, h_ref, route_ref, cnt_ref):
    i = pl.program_id(0)
    tm = x_ref.shape[0]
    half = a_ref.shape[1]

    @pl.when(i == 0)
    def _():
        cnt_ref[...] = cin_ref[...]

    x2 = (x_ref[...]
          + jnp.dot(a_ref[...].astype(BF16), wo_ref[0:half, :], preferred_element_type=F32)
          + jnp.dot(c_ref[...].astype(BF16), wo_ref[half:, :], preferred_element_type=F32))
    x2_ref[...] = x2
    ms = jnp.mean(x2 * x2, axis=-1, keepdims=True)
    h = (x2 * lax.rsqrt(ms + EPS)) * n2_ref[...]
    _store_row_tiles(h_ref, h)

    h_hi = h.astype(BF16)
    h_lo = (h - h_hi.astype(F32)).astype(BF16)
    wr = wr_ref[...]
    w_hi = wr.astype(BF16)
    w_lo = (wr - w_hi.astype(F32)).astype(BF16)
    nt = (((1,), (1,)), ((), ()))
    logits = (lax.dot_general(w_hi, h_hi, nt, preferred_element_type=F32)
              + lax.dot_general(w_hi, h_lo, nt, preferred_element_type=F32)
              + lax.dot_general(w_lo, h_hi, nt, preferred_element_type=F32)) + br_ref[...]
    gl = logits[0:N_GROUPS]
    el = logits[N_GROUPS:N_GROUPS + N_EXPERTS]

    sub8 = lax.broadcasted_iota(jnp.int32, (N_GROUPS, tm), 0)

    def first_argmax(vals):
        top = jnp.max(vals, axis=0, keepdims=True)
        idx = jnp.min(jnp.where(vals == top, sub8, N_GROUPS), axis=0, keepdims=True)
        return top, idx

    gexp = jnp.exp(gl - jnp.max(gl, axis=0, keepdims=True))
    gp = gexp / jnp.sum(gexp, axis=0, keepdims=True)
    g_prob, g_idx = first_argmax(gp)

    el3 = el.reshape(N_GROUPS, EXPERTS_PER_GROUP, tm)
    sel = jnp.zeros((EXPERTS_PER_GROUP, tm), F32)
    for g in range(N_GROUPS):
        sel = sel + jnp.where(g_idx == g, el3[g], 0.0)
    v1, i1 = first_argmax(sel)
    v2, i2 = first_argmax(jnp.where(sub8 == i1, -jnp.inf, sel))
    e2 = jnp.exp(v2 - v1)
    p1 = 1.0 / (1.0 + e2)
    p2 = e2 / (1.0 + e2)
    ea = g_idx * EXPERTS_PER_GROUP + i1
    eb = g_idx * EXPERTS_PER_GROUP + i2

    sub64 = lax.broadcasted_iota(jnp.int32, (N_EXPERTS, tm), 0)
    oh_a = sub64 == ea
    oh_b = sub64 == eb
    oh = jnp.where(oh_a | oh_b, 1.0, 0.0)
    before = jnp.dot(oh.astype(BF16), tri_ref[...], preferred_element_type=F32) + cnt_ref[...]
    rank_a = jnp.sum(jnp.where(oh_a, before, 0.0), axis=0, keepdims=True)
    rank_b = jnp.sum(jnp.where(oh_b, before, 0.0), axis=0, keepdims=True)
    cnt_ref[...] = cnt_ref[...] + jnp.sum(oh, axis=1, keepdims=True)

    route_ref[...] = jnp.concatenate(
        [ea.astype(F32), eb.astype(F32), rank_a, rank_b, g_prob * p1, g_prob * p2,
         jnp.zeros((ROUTE_ROWS - 6, tm), F32)], axis=0)


def _mix(a, c, x2d, wo_b, n2, wr, br, tri, cnt_in):
    t, d = x2d.shape
    tm = MIX_TM
    assert t % tm == 0
    const = lambda shape: pl.BlockSpec(shape, lambda i: (0,) * len(shape))
    return pl.pallas_call(
        _mix_kernel,
        grid=(t // tm,),
        in_specs=[
            pl.BlockSpec((tm, a.shape[1]), lambda i: (i, 0)),
            pl.BlockSpec((tm, c.shape[1]), lambda i: (i, 0)),
            pl.BlockSpec((tm, d), lambda i: (i, 0)),
            const(wo_b.shape), const(n2.shape), const(wr.shape), const(br.shape), const(tri.shape),
            const(cnt_in.shape),
        ],
        out_specs=[
            pl.BlockSpec((tm, d), lambda i: (i, 0)),
            pl.BlockSpec((tm * (d // LANES), LANES), lambda i: (i, 0)),
            pl.BlockSpec((ROUTE_ROWS, tm), lambda i: (0, i)),
            const(cnt_in.shape),
        ],
        out_shape=[
            jax.ShapeDtypeStruct((t, d), F32),
            jax.ShapeDtypeStruct((t * (d // LANES), LANES), F32),
            jax.ShapeDtypeStruct((ROUTE_ROWS, t), F32),
            jax.ShapeDtypeStruct(cnt_in.shape, F32),
        ],
        compiler_params=_cparams(("arbitrary",)),
        name="mix",
    )(a, c, x2d, wo_b, n2, wr, br, tri, cnt_in)


ROW_TM = 512
ROW_UNROLL = 8
EXPERT_BM = 256


def _row_tile(row):
    return pl.ds(pl.multiple_of(row * SUBLANES, SUBLANES), SUBLANES)


def _dispatch_kernel(slots_ref, h_ref, xin_ref, xb_ref, sem):
    del xin_ref
    tm = slots_ref.shape[2]

    def issue(t, carry):
        for j in range(2):
            s = slots_ref[0, j, t]
            pltpu.make_async_copy(h_ref.at[_row_tile(t)], xb_ref.at[_row_tile(s)], sem).start(priority=j)
        return carry

    lax.fori_loop(0, tm, issue, 0, unroll=ROW_UNROLL)

    def drain(t, carry):
        for j in range(2):
            pltpu.make_async_copy(h_ref.at[_row_tile(0)], xb_ref.at[_row_tile(0)], sem).wait()
        return carry

    lax.fori_loop(0, tm, drain, 0, unroll=ROW_UNROLL)


def _dispatch(slots3, h, xb):
    tm = ROW_TM
    assert h.shape[0] % (tm * SUBLANES) == 0 and h.shape[1] == LANES
    return pl.pallas_call(
        _dispatch_kernel,
        grid=(h.shape[0] // (tm * SUBLANES),),
        in_specs=[
            pl.BlockSpec((1, 2, tm), lambda i: (i, 0, 0), memory_space=pltpu.SMEM),
            pl.BlockSpec((tm * SUBLANES, LANES), lambda i: (i, 0)),
            pl.BlockSpec(memory_space=pl.ANY),
        ],
        out_specs=pl.BlockSpec(memory_space=pl.ANY),
        out_shape=jax.ShapeDtypeStruct(xb.shape, xb.dtype),
        scratch_shapes=[pltpu.SemaphoreType.DMA],
        input_output_aliases={2: 0},
        compiler_params=_cparams(("arbitrary",)),
        name="dispatch",
    )(slots3, h, xb)


XB_SLOTS = 3


def _experts_kernel(be_ref, nb_ref, xb_hbm, w1_ref, w3_ref, w2_ref, yb_ref, w1b, w3b, w2b, xbuf, xsem):
    i = pl.program_id(0)
    nblk = pl.num_programs(0)
    bm = EXPERT_BM
    rows = bm * SUBLANES

    def xb_copy(blk, slot):
        src = xb_hbm.at[pl.ds(pl.multiple_of(blk * rows, rows), rows)]
        return pltpu.make_async_copy(src, xbuf.at[slot], xsem.at[slot])

    @pl.when(i == 0)
    def _():
        for blk in range(XB_SLOTS - 1):
            @pl.when(blk < nblk)
            def _():
                xb_copy(blk, blk).start()

    ahead = i + (XB_SLOTS - 1)

    @pl.when(ahead < nblk)
    def _():
        xb_copy(ahead, ahead % XB_SLOTS).start()

    slot = i % XB_SLOTS
    xb_copy(i, slot).wait()
    xb_ref = xbuf.at[slot]

    @pl.when((i == 0) | (be_ref[i] != be_ref[jnp.maximum(i - 1, 0)]))
    def _():
        w1b[...] = w1_ref[0].astype(BF16)
        w3b[...] = w3_ref[0].astype(BF16)
        w2b[...] = w2_ref[0].astype(BF16)

    @pl.when(i < nb_ref[0])
    def _():
        x = _load_row_tiles(xb_ref, bm, SUBLANES).astype(BF16)
        a = jnp.dot(x, w1b[...], preferred_element_type=F32)
        b = jnp.dot(x, w3b[...], preferred_element_type=F32)
        mid = (a * (1.0 / (1.0 + jnp.exp(-a)))) * b
        _store_row_tiles(yb_ref, jnp.dot(mid.astype(BF16), w2b[...], preferred_element_type=F32))

    @pl.when(i >= nb_ref[0])
    def _():
        yb_ref[...] = jnp.zeros(yb_ref.shape, F32)


def _experts(block_expert, n_used, xb, w1, w3, w2):
    bm = EXPERT_BM
    _, d, hid = w1.shape
    rows = bm * SUBLANES
    return pl.pallas_call(
        _experts_kernel,
        grid_spec=pltpu.PrefetchScalarGridSpec(
            num_scalar_prefetch=2,
            grid=(xb.shape[0] // rows,),
            in_specs=[
                pl.BlockSpec(memory_space=pl.ANY),
                pl.BlockSpec((1, d, hid), lambda i, be, nb: (be[i], 0, 0)),
                pl.BlockSpec((1, d, hid), lambda i, be, nb: (be[i], 0, 0)),
                pl.BlockSpec((1, hid, d), lambda i, be, nb: (be[i], 0, 0)),
            ],
            out_specs=pl.BlockSpec((rows, LANES), lambda i, be, nb: (i, 0)),
            scratch_shapes=[
                pltpu.VMEM((d, hid), BF16),
                pltpu.VMEM((d, hid), BF16),
                pltpu.VMEM((hid, d), BF16),
                pltpu.VMEM((XB_SLOTS, rows, LANES), F32),
                pltpu.SemaphoreType.DMA((XB_SLOTS,)),
            ],
        ),
        out_shape=jax.ShapeDtypeStruct(xb.shape, F32),
        compiler_params=_cparams(("arbitrary",)),
        name="experts",
    )(block_expert, n_used, xb, w1, w3, w2)


def _combine_kernel(slots_ref, nslots_ref, x2_ref, g_ref, yb_ref, o_ref, ybuf, sem, *, nt):
    i = pl.program_id(0)
    tm = x2_ref.shape[0]

    def gather(srefs, buf):
        def issue(t, carry):
            for j in range(2):
                s = srefs[0, j, t]
                pltpu.make_async_copy(yb_ref.at[_row_tile(s)], ybuf.at[buf, j, _row_tile(t)],
                                      sem.at[buf]).start(priority=j)
            return carry

        lax.fori_loop(0, tm, issue, 0, unroll=ROW_UNROLL)

    @pl.when(i == 0)
    def _():
        gather(slots_ref, 0)

    if nt > 1:
        @pl.when(i < nt - 1)
        def _():
            gather(nslots_ref, (i + 1) % 2)

    cur = i % 2

    def drain(t, carry):
        for j in range(2):
            pltpu.make_async_copy(yb_ref.at[_row_tile(0)], ybuf.at[cur, j, _row_tile(0)], sem.at[cur]).wait()
        return carry

    lax.fori_loop(0, tm, drain, 0, unroll=ROW_UNROLL)
    g = g_ref[...]
    ya = _load_row_tiles(ybuf.at[cur, 0], tm, SUBLANES)
    yb = _load_row_tiles(ybuf.at[cur, 1], tm, SUBLANES)
    o_ref[...] = x2_ref[...] + (g[:, 0:1] * ya + g[:, 1:2] * yb)


def _combine(slots3, x2, gates, yb):
    t, d = x2.shape
    tm = ROW_TM
    assert t % tm == 0
    nt = t // tm
    return pl.pallas_call(
        functools.partial(_combine_kernel, nt=nt),
        grid=(nt,),
        in_specs=[
            pl.BlockSpec((1, 2, tm), lambda i: (i, 0, 0), memory_space=pltpu.SMEM),
            pl.BlockSpec((1, 2, tm), lambda i: (jnp.minimum(i + 1, nt - 1), 0, 0), memory_space=pltpu.SMEM),
            pl.BlockSpec((tm, d), lambda i: (i, 0)),
            pl.BlockSpec((tm, 2), lambda i: (i, 0)),
            pl.BlockSpec(memory_space=pl.ANY),
        ],
        out_specs=pl.BlockSpec((tm, d), lambda i: (i, 0)),
        out_shape=jax.ShapeDtypeStruct((t, d), F32),
        scratch_shapes=[pltpu.VMEM((2, 2, tm * SUBLANES, LANES), F32), pltpu.SemaphoreType.DMA((2,))],
        compiler_params=_cparams(("arbitrary",)),
        name="combine",
    )(slots3, slots3, x2, gates, yb)


def _t5_bucket(dist):
    n = np.maximum(dist, 0)
    max_exact = N_BUCKETS // 2
    nf = np.maximum(n, 1).astype(np.float32)
    large = max_exact + (np.log(nf / np.float32(max_exact)) / np.float32(math.log(MAX_DISTANCE / max_exact))
                         * np.float32(N_BUCKETS - max_exact)).astype(np.int32)
    large = np.minimum(large, N_BUCKETS - 1)
    return np.where(n < max_exact, n, large)


def _bias_table(rel_bias, d_lo, d_hi):
    dist = np.arange(d_lo, d_hi)
    onehot = (_t5_bucket(dist)[None, :] == np.arange(N_BUCKETS)[:, None]).astype(np.float32)
    tab = jnp.sum(rel_bias.T[:, :, None] * onehot[None], axis=1)
    tab = (tab - rel_bias[N_BUCKETS - 1][:, None]) * LOG2E
    return jnp.where(jnp.asarray(dist >= 0)[None, :], tab, MASKED)


def _toeplitz(v, nr, nc):
    n = nr + nc - 1
    assert v.shape[1] == n
    w = jnp.concatenate([v, jnp.zeros((v.shape[0], 1), v.dtype)], axis=1)
    flat = jnp.tile(w, (1, nr))[:, :nr * n]
    return flat.reshape(v.shape[0], nr, n)[:, :, nr - 1:nr - 1 + nc]


def _layer(xp2, xs2, batch, seq, db, ds, cache_k, cache_v, state_conv, page_table, rel_bias, lam, lam_init, p):
    d = xp2.shape[1]
    w = ATTN_WIDTH
    qwc = (jnp.tile(p["q_norm_w"], 2 * N_HEADS) * (HEAD_DIM ** -0.5 * LOG2E)).reshape(w, 1)
    kw = jnp.tile(p["k_norm_w"], 2 * N_HEADS).reshape(1, w)
    gid = np.arange(w) // HEAD_DIM
    gmat = jnp.asarray((gid[:, None] == gid[None, :]).astype(np.float32) / HEAD_DIM, BF16)
    n1 = p["norm1_w"].reshape(1, d)
    win_b = p["w_in"].astype(BF16)
    lam1 = lam.reshape(1).astype(F32)

    qt_p, kb_p, k32_p, vt_p, v32_p, u_p = _proj(xp2, n1, win_b, qwc, kw, gmat)
    qt_s, _, k32_s, _, v32_s, u_s = _proj(xs2, n1, win_b, qwc, kw, gmat)

    cw = p["conv_w"]
    cb = p["conv_b"].reshape(1, CONV_DIM)
    lnw = p["conv_ln_w"].reshape(1, CONV_DIM)
    lnb = p["conv_ln_b"].reshape(1, CONV_DIM)
    oc_p = _pconv(u_p, batch, seq, cw, cb, lnw, lnb)
    oc_s, conv_s = _sconv(state_conv, u_s.reshape(db, ds, CONV_DIM), cw, cb, lnw, lnb)
    conv_p = u_p.reshape(batch, seq, CONV_DIM)[:, seq - (CONV_W - 1):]

    tq, tk = PATTN_TQ, KV_TILE
    bias_tiles = jnp.stack(
        [_toeplitz(_bias_table(rel_bias, (1 - c) * tk - (tk - 1), (1 - c) * tk + tq), tk, tq)
         for c in range(1 + tq // tk)], axis=1)
    swc = p["subln_w"].reshape(V_DIM, 1)
    a_p = _pattn(lam1, qt_p, kb_p, vt_p, bias_tiles, swc, batch, seq, lam_init)

    nrow = 2 * N_HEADS * ds
    eye = jnp.asarray(np.eye(N_HEADS, dtype=bool))

    def per_head_rows(t3):
        full = jnp.where(eye[:, None, None, None, :], t3[:, None, :, :, None], MASKED)
        full = jnp.broadcast_to(full, (N_HEADS, 2, ds, t3.shape[2], N_HEADS))
        return full.reshape(nrow, t3.shape[2] * N_HEADS)

    hm = per_head_rows(jnp.zeros((N_HEADS, ds, PAGE_SIZE), F32))
    blast = per_head_rows(_toeplitz(_bias_table(rel_bias, 1, PAGE_SIZE + ds)[:, ::-1], ds, PAGE_SIZE))
    bnew = per_head_rows(_toeplitz(_bias_table(rel_bias, -(ds - 1), ds)[:, ::-1], ds, ds))
    swr = p["subln_w"].reshape(1, V_DIM)
    q_s = (qt_s.T.astype(F32)).reshape(db, ds, w)
    kn3 = k32_s.reshape(db, ds * N_HEADS, V_DIM)
    vn3 = v32_s.reshape(db, ds * N_HEADS, V_DIM)
    a_s = _sattn(page_table, lam1, q_s, kn3, vn3, hm, blast, bnew, swr, cache_k, cache_v, lam_init)

    wo_b = p["w_out"].astype(BF16)
    n2 = p["norm2_w"].reshape(1, d)
    wr = jnp.concatenate([p["w_group"], p["w_expert"]], axis=1).T
    br = jnp.concatenate([p["b_group"], p["b_expert"]]).reshape(-1, 1)
    tri = jnp.asarray(np.triu(np.ones((MIX_TM, MIX_TM), np.float32), 1), BF16)
    cnt0 = jnp.zeros((N_EXPERTS, 1), F32)
    x2_p, h_p, route_p, cnt_p = _mix(a_p, oc_p, xp2, wo_b, n2, wr, br, tri, cnt0)
    x2_s, h_s, route_s, cnt = _mix(a_s.reshape(db * ds, w), oc_s.reshape(db * ds, CONV_DIM), xs2,
                                   wo_b, n2, wr, br, tri, cnt_p)

    bm = EXPERT_BM
    counts = cnt[:, 0].astype(jnp.int32)
    padded = (counts + bm - 1) // bm * bm
    pad_end = jnp.cumsum(padded)
    pad_start = pad_end - padded
    n_assign = 2 * (xp2.shape[0] + xs2.shape[0])
    n_blocks = -(-n_assign // bm) + N_EXPERTS
    nslot = n_blocks * bm
    block_first = jnp.arange(n_blocks, dtype=jnp.int32) * bm
    block_expert = jnp.minimum(
        jnp.sum((pad_end[None, :] <= block_first[:, None]).astype(jnp.int32), axis=1), N_EXPERTS - 1)
    n_used = (pad_end[-1] // bm).reshape(1).astype(jnp.int32)
    expert_ids = jnp.arange(N_EXPERTS, dtype=jnp.int32)[:, None, None]

    def slots_of(route):
        e = route[0:2].astype(jnp.int32)
        rank = route[2:4].astype(jnp.int32)
        s = jnp.sum(jnp.where(e[None] == expert_ids, pad_start[:, None, None], 0), axis=0) + rank
        t = s.shape[1]
        return s.reshape(2, t // ROW_TM, ROW_TM).transpose(1, 0, 2)

    slots_p, slots_s = slots_of(route_p), slots_of(route_s)

    assert d == SUBLANES * LANES, "a model row must be exactly one 8x128 tile for the row-tile DMAs"
    xb = jnp.zeros((nslot * SUBLANES, LANES), F32)
    xb = _dispatch(slots_p, h_p, xb)
    xb = _dispatch(slots_s, h_s, xb)
    yb = _experts(block_expert, n_used, xb, p["w1"], p["w3"], p["w2"])
    y_p = _combine(slots_p, x2_p, route_p[4:6].T, yb)
    y_s = _combine(slots_s, x2_s, route_s[4:6].T, yb)

    kshape = (N_HEADS, V_DIM)
    return (y_p, y_s,
            k32_p.reshape(batch, seq, *kshape), v32_p.reshape(batch, seq, *kshape), conv_p,
            k32_s.reshape(db, ds, *kshape), v32_s.reshape(db, ds, *kshape), conv_s)


def kernel(x_prompt, x_sample, cache_k, cache_v, state_conv, page_table, rel_bias, norm1_w, w_in, q_norm_w, k_norm_w, lambda_q1, lambda_k1, lambda_q2, lambda_k2, subln_w, conv_w, conv_b, conv_ln_w, conv_ln_b, w_out, norm2_w, w_group, b_group, w_expert, b_expert, w1, w3, w2):
    batch, seq, d = x_prompt.shape
    db, ds, _ = x_sample.shape
    depth = w_in.shape[0]
    assert depth == 1, "paged caches are read once; a deeper trunk needs one cache slice per layer"
    xp = x_prompt.reshape(batch * seq, d)
    xs = x_sample.reshape(db * ds, d)
    l = 0
    lam_init = 0.8 - 0.6 * math.exp(-0.3 * l)
    lam = (jnp.exp(jnp.sum(lambda_q1[l].astype(F32) * lambda_k1[l].astype(F32)))
           - jnp.exp(jnp.sum(lambda_q2[l].astype(F32) * lambda_k2[l].astype(F32)))
           + lam_init)
    params = dict(norm1_w=norm1_w[l], w_in=w_in[l], q_norm_w=q_norm_w[l], k_norm_w=k_norm_w[l],
                  subln_w=subln_w[l], conv_w=conv_w[l], conv_b=conv_b[l], conv_ln_w=conv_ln_w[l],
                  conv_ln_b=conv_ln_b[l], w_out=w_out[l], norm2_w=norm2_w[l], w_group=w_group[l],
                  b_group=b_group[l], w_expert=w_expert[l], b_expert=b_expert[l],
                  w1=w1[l], w3=w3[l], w2=w2[l])
    y_p, y_s, k_p, v_p, c_p, k_s, v_s, c_s = _layer(
        xp, xs, batch, seq, db, ds, cache_k, cache_v, state_conv[l], page_table, rel_bias, lam, lam_init, params)
    return (y_p.reshape(batch, seq, d), y_s.reshape(db, ds, d),
            k_p[None], v_p[None], c_p[None], k_s[None], v_s[None], c_s[None])
```
